```python
import jax, jax.numpy as jnp
from jax import lax
import numpy as np

D_MODEL = 2048
BATCH = 2
SEQ = 16384
DEPTH = 2

F32 = jnp.float32
N_EVEN = (DEPTH + 1) // 2
N_ODD = DEPTH // 2
EPS = 1e-6
D_FF = 5632
FFN_RES = 0.5

HG_HEADS = 8
HG_DK = 128
HG_DV = 128
HG_WIDTH = HG_HEADS * HG_DK
HG_CHUNK = 64

SSM_INNER = 1024
SSM_HEAD_DIM = 64
SSM_HEADS = SSM_INNER // SSM_HEAD_DIM
SSM_GROUPS = 4
SSM_HPG = SSM_HEADS // SSM_GROUPS
SSM_STATE = 128
SSM_CONV = 4
SSM_CHUNK = 128
SSM_CONV_CH = SSM_INNER + 2 * SSM_GROUPS * SSM_STATE

AB_IN = 4 * HG_WIDTH + SSM_INNER + SSM_CONV_CH + SSM_HEADS
AB_OUT = HG_HEADS * HG_DV + SSM_INNER

ATT_PATTERNS = ((128, 1), (512, 4), (2048, 16))
ATT_GROUPS = 3
ATT_KV_HEADS = 16
ATT_HEAD_DIM = 128
ATT_BLOCK = 128
ATT_Q_WIDTH = ATT_GROUPS * ATT_KV_HEADS * ATT_HEAD_DIM
ATT_KV_WIDTH = ATT_KV_HEADS * ATT_HEAD_DIM
ATT_IN = ATT_Q_WIDTH + 2 * ATT_KV_WIDTH

kernel_name = "hybrid_hgrn2_ssd_dilated_attn_macaron"


def rmsnorm(x, w):
    xf = x.astype(F32)
    y = xf * lax.rsqrt(jnp.mean(xf * xf, axis=-1, keepdims=True) + EPS)
    return (y * w.astype(F32)).astype(x.dtype)


def swiglu(x, w_gate, w_up, w_down):
    return (jax.nn.silu(x @ w_gate) * (x @ w_up)) @ w_down


def hgrn2(q, f_raw, i, g, lb, norm_w):
    B_, S_ = q.shape[:2]
    nc = S_ // HG_CHUNK

    def heads(t):
        return t.reshape(B_, nc, HG_CHUNK, HG_HEADS, -1).transpose(0, 3, 1, 2, 4)

    lbf = lb.astype(F32)
    log_f = jnp.log(lbf + (1.0 - lbf) * jax.nn.sigmoid(f_raw.astype(F32)))
    k = 1.0 - jnp.exp(log_f)
    qh, kh, vh, lfh = heads(q.astype(F32)), heads(k), heads(i.astype(F32)), heads(log_f)
    b = jnp.cumsum(lfh, axis=3)
    b_mid = b[:, :, :, HG_CHUNK // 2:HG_CHUNK // 2 + 1]
    b_last = b[:, :, :, -1:]
    scores = jnp.einsum('bhcld,bhcsd->bhcls', qh * jnp.exp(b - b_mid), kh * jnp.exp(b_mid - b))
    causal = jnp.tril(jnp.ones((HG_CHUNK, HG_CHUNK), bool))
    o_intra = jnp.einsum('bhcls,bhcsv->bhclv', jnp.where(causal, scores, 0.0), vh)
    q_inter = qh * jnp.exp(b)
    k_state = kh * jnp.exp(b_last - b)
    chunk_decay = jnp.exp(b_last[:, :, :, 0, :])

    def step(S, xs):
        qc, kc, vc, dc = xs
        o = jnp.einsum('bhld,bhdv->bhlv', qc, S)
        S = dc[..., None] * S + jnp.einsum('bhsd,bhsv->bhdv', kc, vc)
        return S, o

    S0 = jnp.zeros((B_, HG_HEADS, HG_DK, HG_DV), F32)
    mv = lambda t: jnp.moveaxis(t, 2, 0)
    _, o_inter = lax.scan(step, S0, (mv(q_inter), mv(k_state), mv(vh), mv(chunk_decay)))
    o = o_intra + jnp.moveaxis(o_inter, 0, 2)
    o = o.transpose(0, 2, 3, 1, 4).reshape(B_, S_, HG_HEADS, HG_DV)
    o = o * lax.rsqrt(jnp.mean(o * o, axis=-1, keepdims=True) + EPS)
    o = o.reshape(B_, S_, HG_WIDTH) * norm_w.astype(F32) * jax.nn.silu(g.astype(F32))
    return o.astype(q.dtype)


def causal_depthwise_conv(x, w, bias):
    out = lax.conv_general_dilated(x, w[:, None, :], window_strides=(1,),
                                   padding=[(SSM_CONV - 1, 0)],
                                   dimension_numbers=('NWC', 'WIO', 'NWC'),
                                   feature_group_count=x.shape[-1])
    return out + bias


def mamba2_ssd(z, xbc, dt_raw, conv_w, conv_b, dt_bias, A_log, D, norm_w):
    B_, S_ = z.shape[:2]
    nc = S_ // SSM_CHUNK
    C_, G, R, P, N = SSM_CHUNK, SSM_GROUPS, SSM_HPG, SSM_HEAD_DIM, SSM_STATE
    xbc = jax.nn.silu(causal_depthwise_conv(xbc, conv_w, conv_b))
    xs, Bm, Cm = jnp.split(xbc, [SSM_INNER, SSM_INNER + G * N], axis=-1)
    x = xs.astype(F32).reshape(B_, nc, C_, G, R, P)
    Bm = Bm.astype(F32).reshape(B_, nc, C_, G, N)
    Cm = Cm.astype(F32).reshape(B_, nc, C_, G, N)
    dt = jax.nn.softplus(dt_raw.astype(F32) + dt_bias.astype(F32)).reshape(B_, nc, C_, G, R)
    A = -jnp.exp(A_log.astype(F32)).reshape(G, R)
    a_cs = jnp.cumsum(dt * A, axis=2)
    xdt = x * dt[..., None]
    seg = a_cs[:, :, :, None] - a_cs[:, :, None, :]
    causal = jnp.tril(jnp.ones((C_, C_), bool))[:, :, None, None]
    L = jnp.exp(jnp.where(causal, seg, -jnp.inf))
    CB = jnp.einsum('bclgn,bcsgn->bclsg', Cm, Bm)
    y_diag = jnp.einsum('bclsg,bclsgr,bcsgrp->bclgrp', CB, L, xdt)
    decay_states = jnp.exp(a_cs[:, :, -1:] - a_cs)
    states = jnp.einsum('bcsgn,bcsgr,bcsgrp->bcgrpn', Bm, decay_states, xdt)
    chunk_decay = jnp.exp(a_cs[:, :, -1])

    def step(h, xs_):
        st, dc = xs_
        return h * dc[..., None, None] + st, h

    h0 = jnp.zeros((B_, G, R, P, N), F32)
    _, h_prev = lax.scan(step, h0, (jnp.moveaxis(states, 1, 0), jnp.moveaxis(chunk_decay, 1, 0)))
    h_prev = jnp.moveaxis(h_prev, 0, 1)
    y_off = jnp.einsum('bclgn,bcgrpn,bclgr->bclgrp', Cm, h_prev, jnp.exp(a_cs))
    y = y_diag + y_off + x * D.astype(F32).reshape(G, R)[..., None]
    y = y.reshape(B_, S_, SSM_INNER) * jax.nn.silu(z.astype(F32))
    yg = y.reshape(B_, S_, G, -1)
    yg = yg * lax.rsqrt(jnp.mean(yg * yg, axis=-1, keepdims=True) + EPS)
    return (yg.reshape(B_, S_, SSM_INNER) * norm_w.astype(F32)).astype(z.dtype)


def mixer_ab(h, w_in, w_out, lb, hg_norm_w, conv_w, conv_b, dt_bias, A_log, D, ssm_norm_w):
    proj = h @ w_in
    idx = [int(v) for v in np.cumsum([HG_WIDTH] * 4 + [SSM_INNER, SSM_CONV_CH])]
    hq, hf, hi, hg, z, xbc, dt = jnp.split(proj, idx, axis=-1)
    o_a = hgrn2(hq, hf, hi, hg, lb, hg_norm_w)
    o_b = mamba2_ssd(z, xbc, dt, conv_w, conv_b, dt_bias, A_log, D, ssm_norm_w)
    return jnp.concatenate([o_a, o_b], axis=-1) @ w_out


def dilated_window_attention(q, k, v, dilation, span):
    B_, S_, H_, dh = q.shape
    M = S_ // dilation
    nb = -(-M // ATT_BLOCK)
    Mp = nb * ATT_BLOCK

    def strided(t):
        t = t.reshape(B_, M, dilation, H_, dh).transpose(0, 2, 3, 1, 4)
        t = jnp.pad(t, ((0, 0), (0, 0), (0, 0), (0, Mp - M), (0, 0)))
        return t.reshape(B_, dilation, H_, nb, ATT_BLOCK, dh)

    def with_prev(t):
        prev = jnp.pad(t, ((0, 0), (0, 0), (0, 0), (1, 0), (0, 0), (0, 0)))[:, :, :, :-1]
        return jnp.concatenate([prev, t], axis=4)

    qb = strided(q)
    kc, vc = with_prev(strided(k)), with_prev(strided(v))
    s = jnp.einsum('bdhnqe,bdhnke->bdhnqk', qb, kc).astype(F32) * (ATT_HEAD_DIM ** -0.5)
    qi = jnp.arange(nb)[:, None, None] * ATT_BLOCK + jnp.arange(ATT_BLOCK)[None, :, None]
    ki = (jnp.arange(nb)[:, None, None] - 1) * ATT_BLOCK + jnp.arange(2 * ATT_BLOCK)[None, None, :]
    dist = qi - ki
    allowed = (dist >= 0) & (dist <= span) & (ki >= 0)
    s = jnp.where(allowed, s, -jnp.inf)
    m = jnp.max(s, axis=-1, keepdims=True)
    p = jnp.exp(s - m)
    l = jnp.sum(p, axis=-1, keepdims=True)
    o = jnp.einsum('bdhnqk,bdhnke->bdhnqe', p, vc.astype(F32)) / l
    lse = (m + jnp.log(l))[..., 0]
    o = o.reshape(B_, dilation, H_, Mp, dh)[:, :, :, :M].transpose(0, 3, 1, 2, 4).reshape(B_, S_, H_, dh)
    lse = lse.reshape(B_, dilation, H_, Mp)[..., :M].transpose(0, 3, 1, 2).reshape(B_, S_, H_)
    return o, lse


def mixer_c(h, w_in, w_out):
    B_, S_, _ = h.shape
    proj = h @ w_in
    q, k, v = jnp.split(proj, [ATT_Q_WIDTH, ATT_Q_WIDTH + ATT_KV_WIDTH], axis=-1)
    q = q.reshape(B_, S_, ATT_GROUPS, ATT_KV_HEADS, ATT_HEAD_DIM)
    k = k.reshape(B_, S_, ATT_KV_HEADS, ATT_HEAD_DIM)
    v = v.reshape(B_, S_, ATT_KV_HEADS, ATT_HEAD_DIM)
    outs, lses = [], []
    for grp, (window, dilation) in enumerate(ATT_PATTERNS):
        o_g, lse_g = dilated_window_attention(q[:, :, grp], k, v, dilation, window // dilation)
        outs.append(o_g)
        lses.append(lse_g)
    wts = jax.nn.softmax(jnp.stack(lses, axis=0), axis=0)
    o = jnp.sum(wts[..., None] * jnp.stack(outs, axis=0), axis=0)
    return o.reshape(B_, S_, ATT_KV_WIDTH).astype(h.dtype) @ w_out


def setup_inputs(seed: int = 0) -> dict:
    key = jax.random.key(seed)
    ks = jax.random.split(key, 20)

    def nrm(k, shape, fan_in):
        return jax.random.normal(k, shape, F32) * (fan_in ** -0.5)

    def gain(k, shape):
        return 1.0 + 0.02 * jax.random.normal(k, shape, F32)

    dt0 = jnp.exp(jax.random.uniform(ks[12], (N_EVEN, SSM_HEADS), F32, np.log(1e-3), np.log(1e-1)))
    return {
        "x": jax.random.normal(ks[0], (BATCH, SEQ, D_MODEL), F32),
        "norm_pre": gain(ks[1], (DEPTH, 3, D_MODEL)),
        "norm_post": gain(ks[2], (DEPTH, 3, D_MODEL)),
        "ffn_w_gate": nrm(ks[3], (DEPTH, 2, D_MODEL, D_FF), D_MODEL),
        "ffn_w_up": nrm(ks[4], (DEPTH, 2, D_MODEL, D_FF), D_MODEL),
        "ffn_w_down": nrm(ks[5], (DEPTH, 2, D_FF, D_MODEL), D_FF),
        "ab_w_in": nrm(ks[6], (N_EVEN, D_MODEL, AB_IN), D_MODEL),
        "ab_w_out": nrm(ks[7], (N_EVEN, AB_OUT, D_MODEL), AB_OUT),
        "hgrn_lb": 0.1 * jax.random.normal(ks[8], (N_EVEN + 1, HG_WIDTH), F32),
        "hgrn_norm_w": gain(ks[9], (N_EVEN, HG_WIDTH)),
        "ssm_conv_w": nrm(ks[10], (N_EVEN, SSM_CONV, SSM_CONV_CH), SSM_CONV),
        "ssm_conv_b": 0.02 * jax.random.normal(ks[11], (N_EVEN, SSM_CONV_CH), F32),
        "ssm_dt_bias": dt0 + jnp.log(-jnp.expm1(-dt0)),
        "ssm_A_log": jnp.log(jax.random.uniform(ks[13], (N_EVEN, SSM_HEADS), F32, 1.0, 16.0)),
        "ssm_D": 1.0 + 0.1 * jax.random.normal(ks[14], (N_EVEN, SSM_HEADS), F32),
        "ssm_norm_w": gain(ks[15], (N_EVEN, SSM_INNER)),
        "att_w_in": nrm(ks[16], (N_ODD, D_MODEL, ATT_IN), D_MODEL),
        "att_w_out": nrm(ks[17], (N_ODD, ATT_KV_WIDTH, D_MODEL), ATT_KV_WIDTH),
    }


def reference(x, norm_pre, norm_post, ffn_w_gate, ffn_w_up, ffn_w_down, ab_w_in, ab_w_out,
              hgrn_lb, hgrn_norm_w, ssm_conv_w, ssm_conv_b, ssm_dt_bias, ssm_A_log, ssm_D,
              ssm_norm_w, att_w_in, att_w_out):
    lower_bounds = jnp.cumsum(jax.nn.softmax(hgrn_lb.astype(F32), axis=0), axis=0)
    h = x
    for layer in range(DEPTH):
        y = swiglu(rmsnorm(h, norm_pre[layer, 0]), ffn_w_gate[layer, 0], ffn_w_up[layer, 0], ffn_w_down[layer, 0])
        h = h + FFN_RES * rmsnorm(y, norm_post[layer, 0])
        u = rmsnorm(h, norm_pre[layer, 1])
        if layer % 2 == 0:
            e = layer // 2
            y = mixer_ab(u, ab_w_in[e], ab_w_out[e], lower_bounds[e], hgrn_norm_w[e],
                         ssm_conv_w[e], ssm_conv_b[e], ssm_dt_bias[e], ssm_A_log[e], ssm_D[e], ssm_norm_w[e])
        else:
            o_idx = layer // 2
            y = mixer_c(u, att_w_in[o_idx], att_w_out[o_idx])
        h = h + rmsnorm(y, norm_post[layer, 1])
        y = swiglu(rmsnorm(h, norm_pre[layer, 2]), ffn_w_gate[layer, 1], ffn_w_up[layer, 1], ffn_w_down[layer, 1])
        h = h + FFN_RES * rmsnorm(y, norm_post[layer, 2])
    return h
```

```python
import functools

import jax
import jax.numpy as jnp
from jax import lax
from jax.experimental import pallas as pl
from jax.experimental.pallas import tpu as pltpu

F32 = jnp.float32
BF16 = jnp.bfloat16
EPS = 1e-6
FFN_RES = 0.5

HG_HEADS = 8
HG_DK = 128
HG_WIDTH = HG_HEADS * HG_DK
HG_CHUNK = 64

SSM_INNER = 1024
SSM_HEAD_DIM = 64
SSM_HEADS = SSM_INNER // SSM_HEAD_DIM
SSM_GROUPS = 4
SSM_HPG = SSM_HEADS // SSM_GROUPS
SSM_STATE = 128
SSM_CONV = 4
SSM_CHUNK = 128
SSM_CONV_CH = SSM_INNER + 2 * SSM_GROUPS * SSM_STATE

ATT_PATTERNS = ((128, 1), (512, 4), (2048, 16))
ATT_GROUPS = 3
ATT_KV_HEADS = 16
ATT_HEAD_DIM = 128
ATT_BLOCK = 128
ATT_Q_WIDTH = ATT_GROUPS * ATT_KV_HEADS * ATT_HEAD_DIM
ATT_KV_WIDTH = ATT_KV_HEADS * ATT_HEAD_DIM
ATT_IN = ATT_Q_WIDTH + 2 * ATT_KV_WIDTH

LANES = 128
SUBLANES = 8
VMEM_LIMIT_BYTES = 56 * 1024 * 1024

TOKEN_TILE = 512
FF_TILE = 512
PROJ_COL_TILE = 1024
MIXER_ROWS = 512
ATT_ROWS = 512
ATT_HEADS_PER_STEP = 4

AB_MAIN_COLS = 4 * HG_WIDTH + SSM_CONV_CH
AB_AUX_COLS = HG_WIDTH + LANES


def _cparams(semantics):
    return pltpu.CompilerParams(dimension_semantics=semantics, vmem_limit_bytes=VMEM_LIMIT_BYTES)


def _rms(x, w):
    ms = jnp.mean(x * x, axis=-1, keepdims=True)
    return x * lax.rsqrt(ms + EPS) * w


def _silu(x):
    return x * jax.nn.sigmoid(x)


def _dot(a, b):
    return jnp.dot(a, b, preferred_element_type=F32)


def _dot_nt(a, b):
    return lax.dot_general(a, b, (((1,), (1,)), ((), ())), preferred_element_type=F32)


def _dot_tn(a, b):
    return lax.dot_general(a, b, (((0,), (0,)), ((), ())), preferred_element_type=F32)


def _split3(x):
    hi = x.astype(BF16)
    r = x - hi.astype(F32)
    mid = r.astype(BF16)
    lo = (r - mid.astype(F32)).astype(BF16)
    return hi, mid, lo


def _sel_dot(sel, x):
    hi, mid, lo = _split3(x)
    return _dot(sel, hi) + _dot(sel, mid) + _dot(sel, lo)


def _dot_sel(x, sel):
    hi, mid, lo = _split3(x)
    return _dot(hi, sel) + _dot(mid, sel) + _dot(lo, sel)


def _dot_tn_sel(x, sel):
    hi, mid, lo = _split3(x)
    return _dot_tn(hi, sel) + _dot_tn(mid, sel) + _dot_tn(lo, sel)


def _ffn_kernel(x_ref, npre_ref, npost_ref, wg_ref, wu_ref, wd_ref, o_ref, h_ref, acc_ref):
    j = pl.program_id(1)

    @pl.when(j == 0)
    def _():
        h_ref[...] = _rms(x_ref[...], npre_ref[...]).astype(BF16)

    h = h_ref[...]
    g = _dot(h, wg_ref[...])
    u = _dot(h, wu_ref[...])
    part = _dot((_silu(g) * u).astype(BF16), wd_ref[...])

    @pl.when(j == 0)
    def _():
        acc_ref[...] = part

    @pl.when(j > 0)
    def _():
        acc_ref[...] += part

    @pl.when(j == pl.num_programs(1) - 1)
    def _():
        o_ref[...] = x_ref[...] + FFN_RES * _rms(acc_ref[...], npost_ref[...])


def _ffn(x2, npre, npost, wg, wu, wd):
    t, d = x2.shape
    f = wg.shape[1]
    tm, tf = min(TOKEN_TILE, t), FF_TILE
    return pl.pallas_call(
        _ffn_kernel,
        grid=(t // tm, f // tf),
        in_specs=[
            pl.BlockSpec((tm, d), lambda i, j: (i, 0)),
            pl.BlockSpec((1, d), lambda i, j: (0, 0)),
            pl.BlockSpec((1, d), lambda i, j: (0, 0)),
            pl.BlockSpec((d, tf), lambda i, j: (0, j)),
            pl.BlockSpec((d, tf), lambda i, j: (0, j)),
            pl.BlockSpec((tf, d), lambda i, j: (j, 0)),
        ],
        out_specs=pl.BlockSpec((tm, d), lambda i, j: (i, 0)),
        out_shape=jax.ShapeDtypeStruct((t, d), F32),
        scratch_shapes=[pltpu.VMEM((tm, d), BF16), pltpu.VMEM((tm, d), F32)],
        compiler_params=_cparams(("parallel", "arbitrary")),
        name="ffn",
    )(x2, npre.reshape(1, d), npost.reshape(1, d), wg, wu, wd)


def _proj_kernel(x_ref, nw_ref, w_ref, *rest, has_aux):
    if has_aux:
        waux_ref, o_ref, oaux_ref, h_ref = rest
    else:
        o_ref, h_ref = rest

    @pl.when(pl.program_id(1) == 0)
    def _():
        h = _rms(x_ref[...], nw_ref[...]).astype(BF16)
        h_ref[...] = h
        if has_aux:
            oaux_ref[...] = _dot(h, waux_ref[...])

    o_ref[...] = _dot(h_ref[...], w_ref[...]).astype(o_ref.dtype)


def _norm_proj(x2, nw, w_main, w_aux=None):
    t, d = x2.shape
    n = w_main.shape[1]
    tm, tn = min(TOKEN_TILE, t), PROJ_COL_TILE
    has_aux = w_aux is not None
    in_specs = [
        pl.BlockSpec((tm, d), lambda i, j: (i, 0)),
        pl.BlockSpec((1, d), lambda i, j: (0, 0)),
        pl.BlockSpec((d, tn), lambda i, j: (0, j)),
    ]
    out_specs = [pl.BlockSpec((tm, tn), lambda i, j: (i, j))]
    out_shape = [jax.ShapeDtypeStruct((t, n), BF16)]
    args = [x2, nw.reshape(1, d), w_main]
    if has_aux:
        na = w_aux.shape[1]
        in_specs.append(pl.BlockSpec((d, na), lambda i, j: (0, 0)))
        out_specs.append(pl.BlockSpec((tm, na), lambda i, j: (i, 0)))
        out_shape.append(jax.ShapeDtypeStruct((t, na), F32))
        args.append(w_aux)
    return pl.pallas_call(
        functools.partial(_proj_kernel, has_aux=has_aux),
        grid=(t // tm, n // tn),
        in_specs=in_specs,
        out_specs=out_specs,
        out_shape=out_shape,
        scratch_shapes=[pltpu.VMEM((tm, d), BF16)],
        compiler_params=_cparams(("parallel", "arbitrary")),
        name="norm_proj",
    )(*args)


def _outproj_kernel(*refs, n_parts):
    parts = refs[:n_parts]
    ws = refs[n_parts:2 * n_parts]
    npost_ref, x_ref, o_ref = refs[2 * n_parts:]
    y = _dot(parts[0][...], ws[0][...])
    for p_ref, w_ref in zip(parts[1:], ws[1:]):
        y = y + _dot(p_ref[...], w_ref[...])
    o_ref[...] = x_ref[...] + _rms(y, npost_ref[...])


def _outproj_res(parts, w, npost, x2):
    t, d = x2.shape
    tm = min(TOKEN_TILE, t)
    n_parts = len(parts)
    kp = w.shape[0] // n_parts
    in_specs = [pl.BlockSpec((tm, kp), lambda i: (i, 0)) for _ in parts]
    in_specs += [pl.BlockSpec((kp, d), functools.partial(lambda i, p: (p, 0), p=p)) for p in range(n_parts)]
    in_specs += [pl.BlockSpec((1, d), lambda i: (0, 0)), pl.BlockSpec((tm, d), lambda i: (i, 0))]
    return pl.pallas_call(
        functools.partial(_outproj_kernel, n_parts=n_parts),
        grid=(t // tm,),
        in_specs=in_specs,
        out_specs=pl.BlockSpec((tm, d), lambda i: (i, 0)),
        out_shape=jax.ShapeDtypeStruct((t, d), F32),
        compiler_params=_cparams(("parallel",)),
        name="outproj_res",
    )(*parts, *([w] * n_parts), npost.reshape(1, d), x2)


def _hgrn_kernel(q_ref, f_ref, i_ref, g_ref, lb_ref, nw_ref, o_ref, st_ref, *, layer_idx, n_chunks):
    c_len = HG_CHUNK

    @pl.when(pl.program_id(1) == 0)
    def _():
        st_ref[...] = jnp.zeros_like(st_ref)

    lbs = lb_ref[...]
    ex = jnp.exp(lbs - jnp.max(lbs, axis=0, keepdims=True))
    sm = ex / jnp.sum(ex, axis=0, keepdims=True)
    lb = jnp.sum(sm[:layer_idx + 1], axis=0, keepdims=True)
    nw = nw_ref[...]

    row = lax.broadcasted_iota(jnp.int32, (c_len, c_len), 0)
    col = lax.broadcasted_iota(jnp.int32, (c_len, c_len), 1)
    causal = col <= row
    tril = jnp.where(causal, 1.0, 0.0).astype(BF16)

    def chunk(c, carry):
        r0 = pl.multiple_of(c * c_len, c_len)
        rows = pl.ds(r0, c_len)
        f = lb + (1.0 - lb) * jax.nn.sigmoid(f_ref[0, rows, :])
        log_f = jnp.log(f)
        k = 1.0 - f
        b = _sel_dot(tril, log_f)
        b_mid = b[c_len // 2:c_len // 2 + 1, :]
        b_last = b[c_len - 1:c_len, :]
        q = q_ref[0, rows, :].astype(F32)
        v = i_ref[0, rows, :]
        q_intra = (q * jnp.exp(b - b_mid)).astype(BF16)
        k_intra = (k * jnp.exp(b_mid - b)).astype(BF16)
        q_inter = (q * jnp.exp(b)).astype(BF16)
        k_state = (k * jnp.exp(b_last - b)).astype(BF16)
        decay = jnp.exp(b_last)
        outs = []
        for h in range(HG_HEADS):
            sl = slice(h * HG_DK, (h + 1) * HG_DK)
            scores = jnp.where(causal, _dot_nt(q_intra[:, sl], k_intra[:, sl]), 0.0)
            state_t = st_ref[h]
            o = _dot(scores.astype(BF16), v[:, sl]) + _dot_nt(q_inter[:, sl], state_t.astype(BF16))
            st_ref[h] = state_t * decay[:, sl] + _dot_tn(v[:, sl], k_state[:, sl])
            ms = jnp.mean(o * o, axis=-1, keepdims=True)
            outs.append(o * lax.rsqrt(ms + EPS))
        o_all = jnp.concatenate(outs, axis=-1) * nw * _silu(g_ref[0, rows, :].astype(F32))
        o_ref[0, rows, :] = o_all.astype(o_ref.dtype)
        return carry

    lax.fori_loop(0, n_chunks, chunk, 0)


def _hgrn2(main3, aux3, hgrn_lb, norm_w, layer_idx):
    bsz, s, _ = main3.shape
    rows = min(MIXER_ROWS, s)
    w = HG_WIDTH
    blk = lambda cb: pl.BlockSpec((1, rows, w), functools.partial(lambda b, n, cb: (b, n, cb), cb=cb))
    n_lb = hgrn_lb.shape[0]
    return pl.pallas_call(
        functools.partial(_hgrn_kernel, layer_idx=layer_idx, n_chunks=rows // HG_CHUNK),
        grid=(bsz, s // rows),
        in_specs=[
            blk(0),
            blk(0),
            blk(1),
            blk(2),
            pl.BlockSpec((n_lb, w), lambda b, n: (0, 0)),
            pl.BlockSpec((1, w), lambda b, n: (0, 0)),
        ],
        out_specs=pl.BlockSpec((1, rows, w), lambda b, n: (b, n, 0)),
        out_shape=jax.ShapeDtypeStruct((bsz, s, w), BF16),
        scratch_shapes=[pltpu.VMEM((HG_HEADS, HG_DK, HG_DK), F32)],
        compiler_params=_cparams(("parallel", "arbitrary")),
        name="hgrn2",
    )(main3, aux3, main3, main3, hgrn_lb, norm_w.reshape(1, w))


def _softplus(x):
    return jnp.maximum(x, 0.0) + jnp.log1p(jnp.exp(-jnp.abs(x)))


def _ssd_kernel(z_ref, xbc_ref, dt_ref, cw_ref, cb_ref, dtb_ref, alog_ref, d_ref, nw_ref, e_ref,
                o_ref, xp_ref, ht_ref, *, n_chunks):
    c_len, n_st, p_dim = SSM_CHUNK, SSM_STATE, SSM_HEAD_DIM
    halo = SUBLANES
    gw = SSM_HPG * p_dim
    rows_total = n_chunks * c_len

    @pl.when(pl.program_id(1) == 0)
    def _():
        xp_ref[0:halo, :] = jnp.zeros((halo, SSM_CONV_CH), F32)
        ht_ref[...] = jnp.zeros_like(ht_ref)

    @pl.when(pl.program_id(1) > 0)
    def _():
        xp_ref[0:halo, :] = xp_ref[rows_total:rows_total + halo, :]

    xp_ref[halo:, :] = xbc_ref[0].astype(F32)

    row = lax.broadcasted_iota(jnp.int32, (c_len, c_len), 0)
    col = lax.broadcasted_iota(jnp.int32, (c_len, c_len), 1)
    causal = col <= row
    tril = jnp.where(causal, 1.0, 0.0).astype(BF16)
    triu = jnp.where(row <= col, 1.0, 0.0).astype(BF16)
    lane = lax.broadcasted_iota(jnp.int32, (c_len, LANES), 1)
    first_half = lane < p_dim
    expand = e_ref[...]
    neg_a = -jnp.exp(alog_ref[...])
    cw = cw_ref[...]
    cb = cb_ref[...]

    def chunk(c, carry):
        r0 = pl.multiple_of(c * c_len, c_len)
        rows = pl.ds(r0, c_len)
        full = xp_ref[pl.ds(r0, c_len + halo), :]
        conv = cb
        for j in range(SSM_CONV):
            off = halo - (SSM_CONV - 1) + j
            conv = conv + cw[j:j + 1, :] * full[off:off + c_len, :]
        act = _silu(conv)
        xs = act[:, :SSM_INNER]
        b_all = act[:, SSM_INNER:SSM_INNER + SSM_GROUPS * n_st].astype(BF16)
        c_all = act[:, SSM_INNER + SSM_GROUPS * n_st:].astype(BF16)

        dt_h = _softplus(dt_ref[0, rows, :] + dtb_ref[...])
        dta_h = dt_h * neg_a
        acs_h = _sel_dot(tril, dta_h)
        acs_t = _dot_tn_sel(dta_h, triu)
        dt_x = _dot_sel(dt_h, expand)
        acs_x = _dot_sel(acs_h, expand)
        a_last = acs_x[c_len - 1:c_len, :]
        xdt = xs * dt_x
        dec_x = (jnp.exp(a_last - acs_x) * xdt).astype(BF16)
        e_acs = jnp.exp(acs_x)
        chunk_decay = jnp.exp(a_last)

        ys = []
        for g in range(SSM_GROUPS):
            bg = b_all[:, g * n_st:(g + 1) * n_st]
            cg = c_all[:, g * n_st:(g + 1) * n_st]
            cbm = _dot_nt(cg, bg)
            y_pairs = []
            for pair in range(SSM_HPG // 2):
                lo = g * gw + pair * LANES
                x_pair = xdt[:, lo:lo + LANES]
                halves = (jnp.where(first_half, x_pair, 0.0).astype(BF16),
                          jnp.where(first_half, 0.0, x_pair).astype(BF16))
                y_pair = None
                for sub in range(2):
                    hd = g * SSM_HPG + pair * 2 + sub
                    seg = acs_h[:, hd:hd + 1] - acs_t[hd:hd + 1, :]
                    decay_ls = jnp.exp(jnp.where(causal, seg, -jnp.inf))
                    term = _dot((cbm * decay_ls).astype(BF16), halves[sub])
                    y_pair = term if y_pair is None else y_pair + term
                y_pairs.append(y_pair)
            y_diag = jnp.concatenate(y_pairs, axis=-1)
            gs = slice(g * gw, (g + 1) * gw)
            h_t = ht_ref[g]
            y_off = _dot(cg, h_t.astype(BF16)) * e_acs[:, gs]
            ht_ref[g] = h_t * chunk_decay[:, gs] + _dot_tn(bg, dec_x[:, gs])
            ys.append(y_diag + y_off)
        y = jnp.concatenate(ys, axis=-1) + xs * d_ref[...]
        y = y * _silu(z_ref[0, rows, :].astype(F32))
        normed = []
        for g in range(SSM_GROUPS):
            yg = y[:, g * gw:(g + 1) * gw]
            ms = jnp.mean(yg * yg, axis=-1, keepdims=True)
            normed.append(yg * lax.rsqrt(ms + EPS))
        o_ref[0, rows, :] = (jnp.concatenate(normed, axis=-1) * nw_ref[...]).astype(o_ref.dtype)
        return carry

    lax.fori_loop(0, n_chunks, chunk, 0)


def _ssd(main3, aux3, conv_w, conv_b, dt_bias, a_log, d_skip, norm_w):
    bsz, s, _ = main3.shape
    rows = min(MIXER_ROWS, s)
    inner, ch = SSM_INNER, SSM_CONV_CH
    pad = LANES - SSM_HEADS
    dtb = jnp.pad(dt_bias, (0, pad)).reshape(1, LANES)
    alog = jnp.pad(a_log, (0, pad)).reshape(1, LANES)
    d_x = jnp.repeat(d_skip, SSM_HEAD_DIM).reshape(1, inner)
    head_of_lane = jnp.arange(inner) // SSM_HEAD_DIM
    expand = (jnp.arange(LANES)[:, None] == head_of_lane[None, :]).astype(BF16)
    const = lambda shape: pl.BlockSpec(shape, lambda b, n: (0, 0))
    return pl.pallas_call(
        functools.partial(_ssd_kernel, n_chunks=rows // SSM_CHUNK),
        grid=(bsz, s // rows),
        in_specs=[
            pl.BlockSpec((1, rows, inner), lambda b, n: (b, n, 3)),
            pl.BlockSpec((1, rows, ch), lambda b, n: (b, n, 2)),
            pl.BlockSpec((1, rows, LANES), lambda b, n: (b, n, HG_WIDTH // LANES)),
            const((SSM_CONV, ch)), const((1, ch)), const((1, LANES)), const((1, LANES)),
            const((1, inner)), const((1, inner)), const((LANES, inner)),
        ],
        out_specs=pl.BlockSpec((1, rows, inner), lambda b, n: (b, n, 0)),
        out_shape=jax.ShapeDtypeStruct((bsz, s, inner), BF16),
        scratch_shapes=[pltpu.VMEM((rows + SUBLANES, ch), F32),
                        pltpu.VMEM((SSM_GROUPS, SSM_STATE, SSM_HPG * SSM_HEAD_DIM), F32)],
        compiler_params=_cparams(("parallel", "arbitrary")),
        name="ssd",
    )(main3, main3, aux3, conv_w, conv_b.reshape(1, ch), dtb, alog, d_x, norm_w.reshape(1, inner), expand)


def _attn_kernel(q_ref, k_ref, v_ref, kp_ref, vp_ref, o_ref, st_ref, *, heads, n_qblocks, span):
    blk, dh = ATT_BLOCK, ATT_HEAD_DIM
    n = pl.program_id(2)
    hq = pl.program_id(3)
    scale = dh ** -0.5
    row = lax.broadcasted_iota(jnp.int32, (blk, blk), 0)
    col = lax.broadcasted_iota(jnp.int32, (blk, blk), 1)
    dist_cur = row - col
    dist_prev = dist_cur + blk
    mask_cur = (dist_cur >= 0) & (dist_cur <= span)
    mask_prev_static = (dist_prev >= 0) & (dist_prev <= span)
    lane = lax.broadcasted_iota(jnp.int32, (blk, LANES), 1)

    @pl.when(hq == 0)
    def _():
        st_ref[...] = jnp.zeros_like(st_ref)

    for h in range(heads):
        sl = slice(h * dh, (h + 1) * dh)
        for j in range(n_qblocks):
            rows = slice(j * blk, (j + 1) * blk)
            q = q_ref[0, rows, sl]
            kc = k_ref[0, rows, sl]
            vc = v_ref[0, rows, sl]
            if j == 0:
                kp = kp_ref[0, :, sl]
                vp = vp_ref[0, :, sl]
                mask_prev = mask_prev_static & (n > 0)
            else:
                prows = slice((j - 1) * blk, j * blk)
                kp = k_ref[0, prows, sl]
                vp = v_ref[0, prows, sl]
                mask_prev = mask_prev_static
            s_p = jnp.where(mask_prev, _dot_nt(q, kp) * scale, -jnp.inf)
            s_c = jnp.where(mask_cur, _dot_nt(q, kc) * scale, -jnp.inf)
            m = jnp.maximum(jnp.max(s_p, axis=-1, keepdims=True), jnp.max(s_c, axis=-1, keepdims=True))
            p_p = jnp.exp(s_p - m)
            p_c = jnp.exp(s_c - m)
            l = jnp.sum(p_p, axis=-1, keepdims=True) + jnp.sum(p_c, axis=-1, keepdims=True)
            o = (_dot(p_p.astype(BF16), vp) + _dot(p_c.astype(BF16), vc)) / l
            o_ref[0, rows, sl] = o.astype(o_ref.dtype)
            lse = m + jnp.log(l)
            st_ref[0, rows, :] = jnp.where(lane == hq * heads + h, lse, st_ref[0, rows, :])


def _attn_group(qkv3, grp, window, dilation):
    bsz, s, width = qkv3.shape
    d = dilation
    m = s // d
    span = window // d
    rows = min(ATT_ROWS, m)
    heads = ATT_HEADS_PER_STEP
    bw = heads * ATT_HEAD_DIM
    cols_per_row = width // bw
    hblocks = ATT_KV_HEADS // heads
    q0 = grp * hblocks
    k0 = ATT_Q_WIDTH // bw
    v0 = (ATT_Q_WIDTH + ATT_KV_WIDTH) // bw
    qb_per_step = rows // ATT_BLOCK
    view = qkv3.reshape(bsz, m, d * width)

    def cur(c0):
        return pl.BlockSpec((1, rows, bw), lambda b, r, n, hq: (b, n, r * cols_per_row + c0 + hq))

    def prev(c0):
        return pl.BlockSpec((1, ATT_BLOCK, bw),
                            lambda b, r, n, hq: (b, jnp.maximum(n * qb_per_step - 1, 0), r * cols_per_row + c0 + hq))

    o, st = pl.pallas_call(
        functools.partial(_attn_kernel, heads=heads, n_qblocks=qb_per_step, span=span),
        grid=(bsz, d, m // rows, hblocks),
        in_specs=[cur(q0), cur(k0), cur(v0), prev(k0), prev(v0)],
        out_specs=[
            pl.BlockSpec((1, rows, bw), lambda b, r, n, hq: (b, n, r * hblocks + hq)),
            pl.BlockSpec((1, rows, LANES), lambda b, r, n, hq: (b, n, r)),
        ],
        out_shape=[
            jax.ShapeDtypeStruct((bsz, m, d * ATT_KV_WIDTH), BF16),
            jax.ShapeDtypeStruct((bsz, m, d * LANES), F32),
        ],
        compiler_params=_cparams(("parallel", "parallel", "arbitrary", "arbitrary")),
        name=f"attn_d{d}",
    )(view, view, view, view, view)
    return o.reshape(bsz * s, ATT_KV_WIDTH), st.reshape(bsz * s, LANES)


def _attn_out_kernel(o0_ref, o1_ref, o2_ref, s0_ref, s1_ref, s2_ref, w_ref, npost_ref, x_ref, out_ref, comb_ref):
    o_refs = (o0_ref, o1_ref, o2_ref)
    lses = [s0_ref[...], s1_ref[...], s2_ref[...]]
    mx = jnp.maximum(jnp.maximum(lses[0], lses[1]), lses[2])
    es = [jnp.exp(v - mx) for v in lses]
    inv = 1.0 / (es[0] + es[1] + es[2])
    wts = [e * inv for e in es]
    dh = ATT_HEAD_DIM
    for h in range(ATT_KV_HEADS):
        sl = slice(h * dh, (h + 1) * dh)
        acc = wts[0][:, h:h + 1] * o_refs[0][:, sl].astype(F32)
        for g in (1, 2):
            acc = acc + wts[g][:, h:h + 1] * o_refs[g][:, sl].astype(F32)
        comb_ref[:, sl] = acc.astype(BF16)
    y = _dot(comb_ref[...], w_ref[...])
    out_ref[...] = x_ref[...] + _rms(y, npost_ref[...])


def _attn_out(os, sts, w, npost, x2):
    t, d = x2.shape
    tm = min(TOKEN_TILE, t)
    kw = ATT_KV_WIDTH
    row_blk = lambda width: pl.BlockSpec((tm, width), lambda i: (i, 0))
    return pl.pallas_call(
        _attn_out_kernel,
        grid=(t // tm,),
        in_specs=[row_blk(kw)] * 3 + [row_blk(LANES)] * 3 + [
            pl.BlockSpec((kw, d), lambda i: (0, 0)),
            pl.BlockSpec((1, d), lambda i: (0, 0)),
            row_blk(d),
        ],
        out_specs=row_blk(d),
        out_shape=jax.ShapeDtypeStruct((t, d), F32),
        scratch_shapes=[pltpu.VMEM((tm, kw), BF16)],
        compiler_params=_cparams(("parallel",)),
        name="attn_out",
    )(*os, *sts, w, npost.reshape(1, d), x2)


def _mixer_ab(h2, bsz, s, nw_pre, nw_post, w_in, w_out, hgrn_lb, layer_idx, hg_norm_w,
              conv_w, conv_b, dt_bias, a_log, d_skip, ssm_norm_w):
    d = h2.shape[1]
    c = [HG_WIDTH * k for k in range(1, 5)] + [4 * HG_WIDTH + SSM_INNER, 4 * HG_WIDTH + SSM_INNER + SSM_CONV_CH]
    hq, hf, hi, hg, z, xbc, dt = jnp.split(w_in, c, axis=1)
    w_main = jnp.concatenate([hq, hi, hg, z, xbc], axis=1).astype(BF16)
    w_aux = jnp.concatenate([hf, dt, jnp.zeros((d, LANES - SSM_HEADS), w_in.dtype)], axis=1).astype(BF16)
    main, aux = _norm_proj(h2, nw_pre, w_main, w_aux)
    main3 = main.reshape(bsz, s, AB_MAIN_COLS)
    aux3 = aux.reshape(bsz, s, AB_AUX_COLS)
    o_a = _hgrn2(main3, aux3, hgrn_lb, hg_norm_w, layer_idx)
    o_b = _ssd(main3, aux3, conv_w, conv_b, dt_bias, a_log, d_skip, ssm_norm_w)
    t = bsz * s
    return _outproj_res([o_a.reshape(t, HG_WIDTH), o_b.reshape(t, SSM_INNER)], w_out.astype(BF16), nw_post, h2)


def _mixer_c(h2, bsz, s, nw_pre, nw_post, w_in, w_out):
    (qkv,) = _norm_proj(h2, nw_pre, w_in.astype(BF16))
    qkv3 = qkv.reshape(bsz, s, ATT_IN)
    os, sts = [], []
    for grp, (window, dilation) in enumerate(ATT_PATTERNS):
        o, st = _attn_group(qkv3, grp, window, dilation)
        os.append(o)
        sts.append(st)
    return _attn_out(os, sts, w_out.astype(BF16), nw_post, h2)


def kernel(x, norm_pre, norm_post, ffn_w_gate, ffn_w_up, ffn_w_down, ab_w_in, ab_w_out, hgrn_lb, hgrn_norm_w,
           ssm_conv_w, ssm_conv_b, ssm_dt_bias, ssm_A_log, ssm_D, ssm_norm_w, att_w_in, att_w_out):
    bsz, s, d = x.shape
    depth = norm_pre.shape[0]
    h2 = x.reshape(bsz * s, d)

    def ffn(h2, layer, k):
        return _ffn(h2, norm_pre[layer, 2 * k], norm_post[layer, 2 * k],
                    ffn_w_gate[layer, k].astype(BF16), ffn_w_up[layer, k].astype(BF16),
                    ffn_w_down[layer, k].astype(BF16))

    for layer in range(depth):
        h2 = ffn(h2, layer, 0)
        if layer % 2 == 0:
            e = layer // 2
            h2 = _mixer_ab(h2, bsz, s, norm_pre[layer, 1], norm_post[layer, 1], ab_w_in[e], ab_w_out[e],
                           hgrn_lb, e, hgrn_norm_w[e], ssm_conv_w[e], ssm_conv_b[e], ssm_dt_bias[e],
                           ssm_A_log[e], ssm_D[e], ssm_norm_w[e])
        else:
            o_idx = layer // 2
            h2 = _mixer_c(h2, bsz, s, norm_pre[layer, 1], norm_post[layer, 1], att_w_in[o_idx], att_w_out[o_idx])
        h2 = ffn(h2, layer, 1)
    return h2.reshape(bsz, s, d)
```

```python
import functools

import jax
import jax.numpy as jnp
from jax import lax
from jax.experimental import pallas as pl
from jax.experimental.pallas import tpu as pltpu

F32 = jnp.float32
BF16 = jnp.bfloat16
EPS = 1e-6
FFN_RES = 0.5

HG_HEADS = 8
HG_DK = 128
HG_WIDTH = HG_HEADS * HG_DK
HG_CHUNK = 64

SSM_INNER = 1024
SSM_HEAD_DIM = 64
SSM_HEADS = SSM_INNER // SSM_HEAD_DIM
SSM_GROUPS = 4
SSM_HPG = SSM_HEADS // SSM_GROUPS
SSM_STATE = 128
SSM_CONV = 4
SSM_CHUNK = 128
SSM_CONV_CH = SSM_INNER + 2 * SSM_GROUPS * SSM_STATE

ATT_PATTERNS = ((128, 1), (512, 4), (2048, 16))
ATT_GROUPS = 3
ATT_KV_HEADS = 16
ATT_HEAD_DIM = 128
ATT_BLOCK = 128
ATT_Q_WIDTH = ATT_GROUPS * ATT_KV_HEADS * ATT_HEAD_DIM
ATT_KV_WIDTH = ATT_KV_HEADS * ATT_HEAD_DIM
ATT_IN = ATT_Q_WIDTH + 2 * ATT_KV_WIDTH

LANES = 128
SUBLANES = 8
VMEM_LIMIT_BYTES = 56 * 1024 * 1024

TOKEN_TILE = 512
FF_TILE = 512
PROJ_COL_TILE = 1024
MIXER_ROWS = 512
ATT_ROWS = 256
ATTN_OUT_TILE = 256

AB_MAIN_COLS = 4 * HG_WIDTH + SSM_CONV_CH
AB_AUX_COLS = HG_WIDTH + LANES


def _cparams(semantics):
    return pltpu.CompilerParams(dimension_semantics=semantics, vmem_limit_bytes=VMEM_LIMIT_BYTES)


def _rms(x, w):
    ms = jnp.mean(x * x, axis=-1, keepdims=True)
    return x * lax.rsqrt(ms + EPS) * w


def _silu(x):
    return x * jax.nn.sigmoid(x)


def _dot(a, b):
    return jnp.dot(a, b, preferred_element_type=F32)


def _dot_nt(a, b):
    return lax.dot_general(a, b, (((1,), (1,)), ((), ())), preferred_element_type=F32)


def _dot_tn(a, b):
    return lax.dot_general(a, b, (((0,), (0,)), ((), ())), preferred_element_type=F32)


def _split3(x):
    hi = x.astype(BF16)
    r = x - hi.astype(F32)
    mid = r.astype(BF16)
    lo = (r - mid.astype(F32)).astype(BF16)
    return hi, mid, lo


def _sel_dot(sel, x):
    hi, mid, lo = _split3(x)
    return _dot(sel, hi) + _dot(sel, mid) + _dot(sel, lo)


def _dot_sel(x, sel):
    hi, mid, lo = _split3(x)
    return _dot(hi, sel) + _dot(mid, sel) + _dot(lo, sel)


def _dot_tn_sel(x, sel):
    hi, mid, lo = _split3(x)
    return _dot_tn(hi, sel) + _dot_tn(mid, sel) + _dot_tn(lo, sel)


def _ffn_kernel(x_ref, npre_ref, npost_ref, wg_ref, wu_ref, wd_ref, o_ref, h_ref, acc_ref):
    j = pl.program_id(1)

    @pl.when(j == 0)
    def _():
        h_ref[...] = _rms(x_ref[...], npre_ref[...]).astype(BF16)
        acc_ref[...] = jnp.zeros_like(acc_ref)

    h = h_ref[...]
    g = _dot(h, wg_ref[...])
    u = _dot(h, wu_ref[...])
    acc_ref[...] += _dot((_silu(g) * u).astype(BF16), wd_ref[...])

    @pl.when(j == pl.num_programs(1) - 1)
    def _():
        o_ref[...] = x_ref[...] + FFN_RES * _rms(acc_ref[...], npost_ref[...])


def _ffn(x2, npre, npost, wg, wu, wd):
    t, d = x2.shape
    f = wg.shape[1]
    tm, tf = min(TOKEN_TILE, t), FF_TILE
    return pl.pallas_call(
        _ffn_kernel,
        grid=(t // tm, f // tf),
        in_specs=[
            pl.BlockSpec((tm, d), lambda i, j: (i, 0)),
            pl.BlockSpec((1, d), lambda i, j: (0, 0)),
            pl.BlockSpec((1, d), lambda i, j: (0, 0)),
            pl.BlockSpec((d, tf), lambda i, j: (0, j)),
            pl.BlockSpec((d, tf), lambda i, j: (0, j)),
            pl.BlockSpec((tf, d), lambda i, j: (j, 0)),
        ],
        out_specs=pl.BlockSpec((tm, d), lambda i, j: (i, 0)),
        out_shape=jax.ShapeDtypeStruct((t, d), F32),
        scratch_shapes=[pltpu.VMEM((tm, d), BF16), pltpu.VMEM((tm, d), F32)],
        compiler_params=_cparams(("parallel", "arbitrary")),
        name="ffn",
    )(x2, npre.reshape(1, d), npost.reshape(1, d), wg, wu, wd)


def _proj_kernel(x_ref, nw_ref, w_ref, *rest, has_aux):
    if has_aux:
        waux_ref, o_ref, oaux_ref, h_ref = rest
    else:
        o_ref, h_ref = rest

    @pl.when(pl.program_id(1) == 0)
    def _():
        h = _rms(x_ref[...], nw_ref[...]).astype(BF16)
        h_ref[...] = h
        if has_aux:
            oaux_ref[...] = _dot(h, waux_ref[...])

    o_ref[...] = _dot(h_ref[...], w_ref[...]).astype(o_ref.dtype)


def _norm_proj(x2, nw, w_main, w_aux=None):
    t, d = x2.shape
    n = w_main.shape[1]
    tm, tn = min(TOKEN_TILE, t), PROJ_COL_TILE
    has_aux = w_aux is not None
    in_specs = [
        pl.BlockSpec((tm, d), lambda i, j: (i, 0)),
        pl.BlockSpec((1, d), lambda i, j: (0, 0)),
        pl.BlockSpec((d, tn), lambda i, j: (0, j)),
    ]
    out_specs = [pl.BlockSpec((tm, tn), lambda i, j: (i, j))]
    out_shape = [jax.ShapeDtypeStruct((t, n), BF16)]
    args = [x2, nw.reshape(1, d), w_main]
    if has_aux:
        na = w_aux.shape[1]
        in_specs.append(pl.BlockSpec((d, na), lambda i, j: (0, 0)))
        out_specs.append(pl.BlockSpec((tm, na), lambda i, j: (i, 0)))
        out_shape.append(jax.ShapeDtypeStruct((t, na), F32))
        args.append(w_aux)
    return pl.pallas_call(
        functools.partial(_proj_kernel, has_aux=has_aux),
        grid=(t // tm, n // tn),
        in_specs=in_specs,
        out_specs=out_specs,
        out_shape=out_shape,
        scratch_shapes=[pltpu.VMEM((tm, d), BF16)],
        compiler_params=_cparams(("parallel", "arbitrary")),
        name="norm_proj",
    )(*args)


def _outproj_kernel(*refs, n_parts):
    parts = refs[:n_parts]
    ws = refs[n_parts:2 * n_parts]
    npost_ref, x_ref, o_ref = refs[2 * n_parts:]
    y = _dot(parts[0][...], ws[0][...])
    for p_ref, w_ref in zip(parts[1:], ws[1:]):
        y = y + _dot(p_ref[...], w_ref[...])
    o_ref[...] = x_ref[...] + _rms(y, npost_ref[...])


def _outproj_res(parts, w, npost, x2):
    t, d = x2.shape
    tm = min(TOKEN_TILE, t)
    n_parts = len(parts)
    kp = w.shape[0] // n_parts
    in_specs = [pl.BlockSpec((tm, kp), lambda i: (i, 0)) for _ in parts]
    in_specs += [pl.BlockSpec((kp, d), functools.partial(lambda i, p: (p, 0), p=p)) for p in range(n_parts)]
    in_specs += [pl.BlockSpec((1, d), lambda i: (0, 0)), pl.BlockSpec((tm, d), lambda i: (i, 0))]
    return pl.pallas_call(
        functools.partial(_outproj_kernel, n_parts=n_parts),
        grid=(t // tm,),
        in_specs=in_specs,
        out_specs=pl.BlockSpec((tm, d), lambda i: (i, 0)),
        out_shape=jax.ShapeDtypeStruct((t, d), F32),
        compiler_params=_cparams(("parallel",)),
        name="outproj_res",
    )(*parts, *([w] * n_parts), npost.reshape(1, d), x2)


def _hgrn_kernel(q_ref, f_ref, i_ref, g_ref, lb_ref, nw_ref, o_ref, st_ref, *, layer_idx, n_chunks):
    c_len = HG_CHUNK

    @pl.when(pl.program_id(1) == 0)
    def _():
        st_ref[...] = jnp.zeros_like(st_ref)

    lbs = lb_ref[...]
    ex = jnp.exp(lbs - jnp.max(lbs, axis=0, keepdims=True))
    sm = ex / jnp.sum(ex, axis=0, keepdims=True)
    lb = jnp.sum(sm[:layer_idx + 1], axis=0, keepdims=True)
    nw = nw_ref[...]

    row = lax.broadcasted_iota(jnp.int32, (c_len, c_len), 0)
    col = lax.broadcasted_iota(jnp.int32, (c_len, c_len), 1)
    causal = col <= row
    tril = jnp.where(causal, 1.0, 0.0).astype(BF16)

    def chunk(c, carry):
        r0 = pl.multiple_of(c * c_len, c_len)
        rows = pl.ds(r0, c_len)
        f = lb + (1.0 - lb) * jax.nn.sigmoid(f_ref[0, rows, :])
        log_f = jnp.log(f)
        k = 1.0 - f
        b = _sel_dot(tril, log_f)
        b_mid = b[c_len // 2:c_len // 2 + 1, :]
        b_last = b[c_len - 1:c_len, :]
        q = q_ref[0, rows, :].astype(F32)
        v = i_ref[0, rows, :]
        q_intra = (q * jnp.exp(b - b_mid)).astype(BF16)
        k_intra = (k * jnp.exp(b_mid - b)).astype(BF16)
        q_inter = (q * jnp.exp(b)).astype(BF16)
        k_state = (k * jnp.exp(b_last - b)).astype(BF16)
        decay = jnp.exp(b_last)
        outs = []
        for h in range(HG_HEADS):
            sl = slice(h * HG_DK, (h + 1) * HG_DK)
            scores = jnp.where(causal, _dot_nt(q_intra[:, sl], k_intra[:, sl]), 0.0)
            state_t = st_ref[h]
            o = _dot(scores.astype(BF16), v[:, sl]) + _dot_nt(q_inter[:, sl], state_t.astype(BF16))
            st_ref[h] = state_t * decay[:, sl] + _dot_tn(v[:, sl], k_state[:, sl])
            ms = jnp.mean(o * o, axis=-1, keepdims=True)
            outs.append(o * lax.rsqrt(ms + EPS))
        o_all = jnp.concatenate(outs, axis=-1) * nw * _silu(g_ref[0, rows, :].astype(F32))
        o_ref[0, rows, :] = o_all.astype(o_ref.dtype)
        return carry

    lax.fori_loop(0, n_chunks, chunk, 0)


def _hgrn2(main3, aux3, hgrn_lb, norm_w, layer_idx):
    bsz, s, _ = main3.shape
    rows = min(MIXER_ROWS, s)
    w = HG_WIDTH
    blk = lambda cb: pl.BlockSpec((1, rows, w), functools.partial(lambda b, n, cb: (b, n, cb), cb=cb))
    n_lb = hgrn_lb.shape[0]
    return pl.pallas_call(
        functools.partial(_hgrn_kernel, layer_idx=layer_idx, n_chunks=rows // HG_CHUNK),
        grid=(bsz, s // rows),
        in_specs=[
            blk(0),
            blk(0),
            blk(1),
            blk(2),
            pl.BlockSpec((n_lb, w), lambda b, n: (0, 0)),
            pl.BlockSpec((1, w), lambda b, n: (0, 0)),
        ],
        out_specs=pl.BlockSpec((1, rows, w), lambda b, n: (b, n, 0)),
        out_shape=jax.ShapeDtypeStruct((bsz, s, w), BF16),
        scratch_shapes=[pltpu.VMEM((HG_HEADS, HG_DK, HG_DK), F32)],
        compiler_params=_cparams(("parallel", "arbitrary")),
        name="hgrn2",
    )(main3, aux3, main3, main3, hgrn_lb, norm_w.reshape(1, w))


def _softplus(x):
    return jnp.maximum(x, 0.0) + jnp.log1p(jnp.exp(-jnp.abs(x)))


def _ssd_kernel(z_ref, xbc_ref, dt_ref, cw_ref, cb_ref, dtb_ref, alog_ref, d_ref, nw_ref, e_ref,
                o_ref, xp_ref, ht_ref, *, n_chunks):
    c_len, n_st, p_dim = SSM_CHUNK, SSM_STATE, SSM_HEAD_DIM
    halo = SUBLANES
    gw = SSM_HPG * p_dim
    rows_total = n_chunks * c_len

    @pl.when(pl.program_id(1) == 0)
    def _():
        xp_ref[0:halo, :] = jnp.zeros((halo, SSM_CONV_CH), F32)
        ht_ref[...] = jnp.zeros_like(ht_ref)

    @pl.when(pl.program_id(1) > 0)
    def _():
        xp_ref[0:halo, :] = xp_ref[rows_total:rows_total + halo, :]

    xp_ref[halo:, :] = xbc_ref[0].astype(F32)

    row = lax.broadcasted_iota(jnp.int32, (c_len, c_len), 0)
    col = lax.broadcasted_iota(jnp.int32, (c_len, c_len), 1)
    causal = col <= row
    tril = jnp.where(causal, 1.0, 0.0).astype(BF16)
    triu = jnp.where(row <= col, 1.0, 0.0).astype(BF16)
    lane = lax.broadcasted_iota(jnp.int32, (c_len, LANES), 1)
    first_half = lane < p_dim
    expand = e_ref[...]
    neg_a = -jnp.exp(alog_ref[...])
    cw = cw_ref[...]
    cb = cb_ref[...]

    def chunk(c, carry):
        r0 = pl.multiple_of(c * c_len, c_len)
        rows = pl.ds(r0, c_len)
        full = xp_ref[pl.ds(r0, c_len + halo), :]
        conv = cb
        for j in range(SSM_CONV):
            off = halo - (SSM_CONV - 1) + j
            conv = conv + cw[j:j + 1, :] * full[off:off + c_len, :]
        act = _silu(conv)
        xs = act[:, :SSM_INNER]
        b_all = act[:, SSM_INNER:SSM_INNER + SSM_GROUPS * n_st].astype(BF16)
        c_all = act[:, SSM_INNER + SSM_GROUPS * n_st:].astype(BF16)

        dt_h = _softplus(dt_ref[0, rows, :] + dtb_ref[...])
        dta_h = dt_h * neg_a
        acs_h = _sel_dot(tril, dta_h)
        acs_t = _dot_tn_sel(dta_h, triu)
        dt_x = _dot_sel(dt_h, expand)
        acs_x = _dot_sel(acs_h, expand)
        a_last = acs_x[c_len - 1:c_len, :]
        xdt = xs * dt_x
        dec_x = (jnp.exp(a_last - acs_x) * xdt).astype(BF16)
        e_acs = jnp.exp(acs_x)
        chunk_decay = jnp.exp(a_last)

        ys = []
        for g in range(SSM_GROUPS):
            bg = b_all[:, g * n_st:(g + 1) * n_st]
            cg = c_all[:, g * n_st:(g + 1) * n_st]
            cbm = _dot_nt(cg, bg)
            y_pairs = []
            for pair in range(SSM_HPG // 2):
                lo = g * gw + pair * LANES
                x_pair = xdt[:, lo:lo + LANES]
                halves = (jnp.where(first_half, x_pair, 0.0).astype(BF16),
                          jnp.where(first_half, 0.0, x_pair).astype(BF16))
                y_pair = None
                for sub in range(2):
                    hd = g * SSM_HPG + pair * 2 + sub
                    seg = acs_h[:, hd:hd + 1] - acs_t[hd:hd + 1, :]
                    decay_ls = jnp.exp(jnp.where(causal, seg, -jnp.inf))
                    term = _dot((cbm * decay_ls).astype(BF16), halves[sub])
                    y_pair = term if y_pair is None else y_pair + term
                y_pairs.append(y_pair)
            y_diag = jnp.concatenate(y_pairs, axis=-1)
            gs = slice(g * gw, (g + 1) * gw)
            h_t = ht_ref[g]
            y_off = _dot(cg, h_t.astype(BF16)) * e_acs[:, gs]
            ht_ref[g] = h_t * chunk_decay[:, gs] + _dot_tn(bg, dec_x[:, gs])
            ys.append(y_diag + y_off)
        y = jnp.concatenate(ys, axis=-1) + xs * d_ref[...]
        y = y * _silu(z_ref[0, rows, :].astype(F32))
        normed = []
        for g in range(SSM_GROUPS):
            yg = y[:, g * gw:(g + 1) * gw]
            ms = jnp.mean(yg * yg, axis=-1, keepdims=True)
            normed.append(yg * lax.rsqrt(ms + EPS))
        o_ref[0, rows, :] = (jnp.concatenate(normed, axis=-1) * nw_ref[...]).astype(o_ref.dtype)
        return carry

    lax.fori_loop(0, n_chunks, chunk, 0)


def _ssd(main3, aux3, conv_w, conv_b, dt_bias, a_log, d_skip, norm_w):
    bsz, s, _ = main3.shape
    rows = min(MIXER_ROWS, s)
    inner, ch = SSM_INNER, SSM_CONV_CH
    pad = LANES - SSM_HEADS
    dtb = jnp.pad(dt_bias, (0, pad)).reshape(1, LANES)
    alog = jnp.pad(a_log, (0, pad)).reshape(1, LANES)
    d_x = jnp.repeat(d_skip, SSM_HEAD_DIM).reshape(1, inner)
    head_of_lane = jnp.arange(inner) // SSM_HEAD_DIM
    expand = (jnp.arange(LANES)[:, None] == head_of_lane[None, :]).astype(BF16)
    const = lambda shape: pl.BlockSpec(shape, lambda b, n: (0, 0))
    return pl.pallas_call(
        functools.partial(_ssd_kernel, n_chunks=rows // SSM_CHUNK),
        grid=(bsz, s // rows),
        in_specs=[
            pl.BlockSpec((1, rows, inner), lambda b, n: (b, n, 3)),
            pl.BlockSpec((1, rows, ch), lambda b, n: (b, n, 2)),
            pl.BlockSpec((1, rows, LANES), lambda b, n: (b, n, HG_WIDTH // LANES)),
            const((SSM_CONV, ch)), const((1, ch)), const((1, LANES)), const((1, LANES)),
            const((1, inner)), const((1, inner)), const((LANES, inner)),
        ],
        out_specs=pl.BlockSpec((1, rows, inner), lambda b, n: (b, n, 0)),
        out_shape=jax.ShapeDtypeStruct((bsz, s, inner), BF16),
        scratch_shapes=[pltpu.VMEM((rows + SUBLANES, ch), F32),
                        pltpu.VMEM((SSM_GROUPS, SSM_STATE, SSM_HPG * SSM_HEAD_DIM), F32)],
        compiler_params=_cparams(("parallel", "arbitrary")),
        name="ssd",
    )(main3, main3, aux3, conv_w, conv_b.reshape(1, ch), dtb, alog, d_x, norm_w.reshape(1, inner), expand)


def _qkv_kernel(x_ref, nw_ref, w_ref, *refs, tm, tn, dilations):
    n_grp = len(dilations)
    q_refs, k_refs, v_refs = refs[:n_grp], refs[n_grp:2 * n_grp], refs[2 * n_grp:3 * n_grp]
    h_ref, acc_ref = refs[3 * n_grp:]
    j = pl.program_id(2)
    per = ATT_KV_WIDTH // tn
    n_slabs = tn // LANES

    @pl.when(j == 0)
    def _():
        h_ref[...] = _rms(x_ref[...], nw_ref[...]).astype(BF16)

    def project_and_emit(targets, scale):
        acc = _dot(h_ref[...], w_ref[...])
        if scale != 1.0:
            acc = acc * scale
        if any(d > 1 for _, d in targets):
            for c in range(n_slabs):
                acc_ref[c] = acc[:, c * LANES:(c + 1) * LANES]
        for o_ref, d in targets:
            if d == 1:
                o_ref[0] = acc.astype(BF16)
                continue
            rows = tm // d
            for r in range(d):
                for c in range(n_slabs):
                    o_ref[r, :, c * LANES:(c + 1) * LANES] = acc_ref[c, pl.ds(r, rows, stride=d), :].astype(BF16)

    for g, d in enumerate(dilations):
        @pl.when((j >= g * per) & (j < (g + 1) * per))
        def _(g=g, d=d):
            project_and_emit([(q_refs[g], d)], ATT_HEAD_DIM ** -0.5)

    @pl.when((j >= n_grp * per) & (j < (n_grp + 1) * per))
    def _():
        project_and_emit(list(zip(k_refs, dilations)), 1.0)

    @pl.when(j >= (n_grp + 1) * per)
    def _():
        project_and_emit(list(zip(v_refs, dilations)), 1.0)


def _qkv_proj(x3, nw, w):
    bsz, s, dm = x3.shape
    tm, tn = min(TOKEN_TILE, s), PROJ_COL_TILE
    per = ATT_KV_WIDTH // tn
    dilations = tuple(d for _, d in ATT_PATTERNS)
    n_grp = len(dilations)

    def out_spec(d, j0):
        return pl.BlockSpec((None, d, tm // d, tn),
                            lambda b, i, j: (b, 0, i, jnp.clip(j - j0, 0, per - 1)))

    starts = [g * per for g in range(n_grp)] + [n_grp * per] * n_grp + [(n_grp + 1) * per] * n_grp
    out_specs = [out_spec(d, j0) for d, j0 in zip(dilations * 3, starts)]
    out_shape = [jax.ShapeDtypeStruct((bsz, d, s // d, ATT_KV_WIDTH), BF16) for d in dilations * 3]
    outs = pl.pallas_call(
        functools.partial(_qkv_kernel, tm=tm, tn=tn, dilations=dilations),
        grid=(bsz, s // tm, w.shape[1] // tn),
        in_specs=[
            pl.BlockSpec((None, tm, dm), lambda b, i, j: (b, i, 0)),
            pl.BlockSpec((1, dm), lambda b, i, j: (0, 0)),
            pl.BlockSpec((dm, tn), lambda b, i, j: (0, j)),
        ],
        out_specs=out_specs,
        out_shape=out_shape,
        scratch_shapes=[pltpu.VMEM((tm, dm), BF16), pltpu.VMEM((tn // LANES, tm, LANES), F32)],
        compiler_params=_cparams(("parallel", "parallel", "arbitrary")),
        name="qkv_proj",
    )(x3, nw.reshape(1, dm), w)
    return outs[:n_grp], outs[n_grp:2 * n_grp], outs[2 * n_grp:]


def _attn_kernel(q_ref, k_ref, v_ref, kp_ref, vp_ref, o_ref, st_ref, *, n_qblocks, span):
    blk, dh = ATT_BLOCK, ATT_HEAD_DIM
    n = pl.program_id(2)
    row = lax.broadcasted_iota(jnp.int32, (blk, 2 * blk), 0)
    col = lax.broadcasted_iota(jnp.int32, (blk, 2 * blk), 1)
    dist = row + blk - col
    mask = (dist >= 0) & (dist <= span)
    mask_first = mask & ((col >= blk) | (n > 0))
    lane = lax.broadcasted_iota(jnp.int32, (blk, LANES), 1)
    ones = jnp.ones((2 * blk, dh), BF16)

    for j in range(n_qblocks):
        rows = slice(j * blk, (j + 1) * blk)
        tile = jnp.zeros((blk, LANES), F32)
        for h in range(ATT_KV_HEADS):
            sl = slice(h * dh, (h + 1) * dh)
            q = q_ref[rows, sl]
            if j == 0:
                k2 = jnp.concatenate([kp_ref[:, sl], k_ref[rows, sl]], axis=0)
                v2 = jnp.concatenate([vp_ref[:, sl], v_ref[rows, sl]], axis=0)
            else:
                rows2 = slice((j - 1) * blk, (j + 1) * blk)
                k2 = k_ref[rows2, sl]
                v2 = v_ref[rows2, sl]
            s = jnp.where(mask_first if j == 0 else mask, _dot_nt(q, k2), -jnp.inf)
            m = jnp.max(s, axis=-1, keepdims=True)
            p = jnp.exp(s - m).astype(BF16)
            ov = _dot(p, jnp.concatenate([v2, ones], axis=1))
            l = ov[:, dh:]
            o_ref[rows, sl] = (ov[:, :dh] / l).astype(o_ref.dtype)
            tile = jnp.where(lane == h, m + jnp.log(l), tile)
        st_ref[rows, :] = tile


def _attn_group(q, k, v, span):
    bsz, d, m, w = q.shape
    rows = min(ATT_ROWS, m)
    qb_per_step = rows // ATT_BLOCK
    cur = pl.BlockSpec((None, None, rows, w), lambda b, r, n: (b, r, n, 0))
    prev = pl.BlockSpec((None, None, ATT_BLOCK, w), lambda b, r, n: (b, r, jnp.maximum(n * qb_per_step - 1, 0), 0))
    return pl.pallas_call(
        functools.partial(_attn_kernel, n_qblocks=qb_per_step, span=span),
        grid=(bsz, d, m // rows),
        in_specs=[cur, cur, cur, prev, prev],
        out_specs=[cur, pl.BlockSpec((None, None, rows, LANES), lambda b, r, n: (b, r, n, 0))],
        out_shape=[jax.ShapeDtypeStruct((bsz, d, m, w), BF16), jax.ShapeDtypeStruct((bsz, d, m, LANES), F32)],
        compiler_params=_cparams(("parallel", "parallel", "arbitrary")),
        name=f"attn_d{d}",
    )(q, k, v, k, v)


def _attn_out_kernel(*refs, tm, dilations):
    n_grp = len(dilations)
    o_refs, s_refs = refs[:n_grp], refs[n_grp:2 * n_grp]
    w_ref, npost_ref, x_ref, out_ref, on_ref, sn_ref, comb_ref = refs[2 * n_grp:]
    dh = ATT_HEAD_DIM

    for g, d in enumerate(dilations):
        rows = tm // d
        for r in range(d):
            if d == 1:
                sn_ref[g] = s_refs[g][r]
            else:
                sn_ref[g, pl.ds(r, rows, stride=d), :] = s_refs[g][r]
                for h in range(ATT_KV_HEADS):
                    on_ref[g, h, pl.ds(r, rows, stride=d), :] = o_refs[g][r, :, h * dh:(h + 1) * dh].astype(F32)

    lses = [sn_ref[g] for g in range(n_grp)]
    mx = functools.reduce(jnp.maximum, lses)
    es = [jnp.exp(v - mx) for v in lses]
    inv = 1.0 / functools.reduce(lambda a, b: a + b, es)
    wts = [e * inv for e in es]
    y = None
    for h in range(ATT_KV_HEADS):
        sl = slice(h * dh, (h + 1) * dh)
        acc = None
        for g, d in enumerate(dilations):
            og = o_refs[g][0, :, sl].astype(F32) if d == 1 else on_ref[g, h]
            term = wts[g][:, h:h + 1] * og
            acc = term if acc is None else acc + term
        comb_ref[:, sl] = acc.astype(BF16)
        if h % 2 == 1:
            pair = slice((h - 1) * dh, (h + 1) * dh)
            part = _dot(comb_ref[:, pair], w_ref[pair, :])
            y = part if y is None else y + part
    out_ref[...] = x_ref[...] + _rms(y, npost_ref[...])


def _attn_out(os, sts, w, npost, x3):
    bsz, s, dm = x3.shape
    tm = min(ATTN_OUT_TILE, s)
    kw = ATT_KV_WIDTH
    dilations = tuple(o.shape[1] for o in os)
    n_grp = len(dilations)
    grp_blk = lambda d, width: pl.BlockSpec((None, d, tm // d, width), lambda b, i: (b, 0, i, 0))
    row_blk = pl.BlockSpec((None, tm, dm), lambda b, i: (b, i, 0))
    return pl.pallas_call(
        functools.partial(_attn_out_kernel, tm=tm, dilations=dilations),
        grid=(bsz, s // tm),
        in_specs=[grp_blk(d, kw) for d in dilations] + [grp_blk(d, LANES) for d in dilations] + [
            pl.BlockSpec((kw, dm), lambda b, i: (0, 0)),
            pl.BlockSpec((1, dm), lambda b, i: (0, 0)),
            row_blk,
        ],
        out_specs=row_blk,
        out_shape=jax.ShapeDtypeStruct((bsz, s, dm), F32),
        scratch_shapes=[pltpu.VMEM((n_grp, ATT_KV_HEADS, tm, LANES), F32),
                        pltpu.VMEM((n_grp, tm, LANES), F32),
                        pltpu.VMEM((tm, kw), BF16)],
        compiler_params=_cparams(("parallel", "parallel")),
        name="attn_out",
    )(*os, *sts, w, npost.reshape(1, dm), x3)


def _mixer_ab(h2, bsz, s, nw_pre, nw_post, w_in, w_out, hgrn_lb, layer_idx, hg_norm_w,
              conv_w, conv_b, dt_bias, a_log, d_skip, ssm_norm_w):
    d = h2.shape[1]
    c = [HG_WIDTH * k for k in range(1, 5)] + [4 * HG_WIDTH + SSM_INNER, 4 * HG_WIDTH + SSM_INNER + SSM_CONV_CH]
    hq, hf, hi, hg, z, xbc, dt = jnp.split(w_in, c, axis=1)
    w_main = jnp.concatenate([hq, hi, hg, z, xbc], axis=1).astype(BF16)
    w_aux = jnp.concatenate([hf, dt, jnp.zeros((d, LANES - SSM_HEADS), w_in.dtype)], axis=1).astype(BF16)
    main, aux = _norm_proj(h2, nw_pre, w_main, w_aux)
    main3 = main.reshape(bsz, s, AB_MAIN_COLS)
    aux3 = aux.reshape(bsz, s, AB_AUX_COLS)
    o_a = _hgrn2(main3, aux3, hgrn_lb, hg_norm_w, layer_idx)
    o_b = _ssd(main3, aux3, conv_w, conv_b, dt_bias, a_log, d_skip, ssm_norm_w)
    t = bsz * s
    return _outproj_res([o_a.reshape(t, HG_WIDTH), o_b.reshape(t, SSM_INNER)], w_out.astype(BF16), nw_post, h2)


def _mixer_c(h2, bsz, s, nw_pre, nw_post, w_in, w_out):
    dm = h2.shape[1]
    x3 = h2.reshape(bsz, s, dm)
    qs, ks, vs = _qkv_proj(x3, nw_pre, w_in.astype(BF16))
    os, sts = [], []
    for g, (window, dilation) in enumerate(ATT_PATTERNS):
        o, st = _attn_group(qs[g], ks[g], vs[g], window // dilation)
        os.append(o)
        sts.append(st)
    return _attn_out(os, sts, w_out.astype(BF16), nw_post, x3).reshape(bsz * s, dm)


def kernel(x, norm_pre, norm_post, ffn_w_gate, ffn_w_up, ffn_w_down, ab_w_in, ab_w_out, hgrn_lb, hgrn_norm_w,
           ssm_conv_w, ssm_conv_b, ssm_dt_bias, ssm_A_log, ssm_D, ssm_norm_w, att_w_in, att_w_out):
    bsz, s, d = x.shape
    depth = norm_pre.shape[0]
    h2 = x.reshape(bsz * s, d)

    def ffn(h2, layer, k):
        return _ffn(h2, norm_pre[layer, 2 * k], norm_post[layer, 2 * k],
                    ffn_w_gate[layer, k].astype(BF16), ffn_w_up[layer, k].astype(BF16),
                    ffn_w_down[layer, k].astype(BF16))

    for layer in range(depth):
        h2 = ffn(h2, layer, 0)
        if layer % 2 == 0:
            e = layer // 2
            h2 = _mixer_ab(h2, bsz, s, norm_pre[layer, 1], norm_post[layer, 1], ab_w_in[e], ab_w_out[e],
                           hgrn_lb, e, hgrn_norm_w[e], ssm_conv_w[e], ssm_conv_b[e], ssm_dt_bias[e],
                           ssm_A_log[e], ssm_D[e], ssm_norm_w[e])
        else:
            o_idx = layer // 2
            h2 = _mixer_c(h2, bsz, s, norm_pre[layer, 1], norm_post[layer, 1], att_w_in[o_idx], att_w_out[o_idx])
        h2 = ffn(h2, layer, 1)
    return h2.reshape(bsz, s, d)
```

```python
import functools

import jax
import jax.numpy as jnp
from jax import lax
from jax.experimental import pallas as pl
from jax.experimental.pallas import tpu as pltpu

F32 = jnp.float32
BF16 = jnp.bfloat16
EPS = 1e-6
FFN_RES = 0.5

HG_HEADS = 8
HG_DK = 128
HG_WIDTH = HG_HEADS * HG_DK
HG_CHUNK = 64

SSM_INNER = 1024
SSM_HEAD_DIM = 64
SSM_HEADS = SSM_INNER // SSM_HEAD_DIM
SSM_GROUPS = 4
SSM_HPG = SSM_HEADS // SSM_GROUPS
SSM_STATE = 128
SSM_CONV = 4
SSM_CHUNK = 128
SSM_CONV_CH = SSM_INNER + 2 * SSM_GROUPS * SSM_STATE

ATT_PATTERNS = ((128, 1), (512, 4), (2048, 16))
ATT_GROUPS = 3
ATT_KV_HEADS = 16
ATT_HEAD_DIM = 128
ATT_BLOCK = 128
ATT_Q_WIDTH = ATT_GROUPS * ATT_KV_HEADS * ATT_HEAD_DIM
ATT_KV_WIDTH = ATT_KV_HEADS * ATT_HEAD_DIM
ATT_IN = ATT_Q_WIDTH + 2 * ATT_KV_WIDTH

LANES = 128
SUBLANES = 8
VMEM_LIMIT_BYTES = 56 * 1024 * 1024
FFN_VMEM_LIMIT_BYTES = 60 * 1024 * 1024

TOKEN_TILE = 512
FFN_TOKEN_TILE = 1024
PROJ_TOKEN_TILE = 1024
QKV_COL_TILE = 512
NORM_ROWS = 128
FF_TILE = 512
PROJ_COL_TILE = 1024
MIXER_ROWS = 512
ATT_ROWS = 512
ATTN_OUT_TILE = 256

AB_MAIN_COLS = 4 * HG_WIDTH + SSM_CONV_CH
AB_AUX_COLS = HG_WIDTH + LANES


def _cparams(semantics, vmem_limit_bytes=VMEM_LIMIT_BYTES):
    return pltpu.CompilerParams(dimension_semantics=semantics, vmem_limit_bytes=vmem_limit_bytes)


def _rms(x, w):
    ms = jnp.mean(x * x, axis=-1, keepdims=True)
    return x * lax.rsqrt(ms + EPS) * w


def _silu(x):
    return x * jax.nn.sigmoid(x)


def _dot(a, b):
    return jnp.dot(a, b, preferred_element_type=F32)


def _dot_nt(a, b):
    return lax.dot_general(a, b, (((1,), (1,)), ((), ())), preferred_element_type=F32)


def _dot_tn(a, b):
    return lax.dot_general(a, b, (((0,), (0,)), ((), ())), preferred_element_type=F32)


def _split3(x):
    hi = x.astype(BF16)
    r = x - hi.astype(F32)
    mid = r.astype(BF16)
    lo = (r - mid.astype(F32)).astype(BF16)
    return hi, mid, lo


def _sel_dot(sel, x):
    hi, mid, lo = _split3(x)
    return _dot(sel, hi) + _dot(sel, mid) + _dot(sel, lo)


def _dot_sel(x, sel):
    hi, mid, lo = _split3(x)
    return _dot(hi, sel) + _dot(mid, sel) + _dot(lo, sel)


def _dot_tn_sel(x, sel):
    hi, mid, lo = _split3(x)
    return _dot_tn(hi, sel) + _dot_tn(mid, sel) + _dot_tn(lo, sel)


def _ffn_kernel(x_ref, npre_ref, npost_ref, wg_ref, wu_ref, wd_ref, o_ref, h_ref):
    j = pl.program_id(1)

    @pl.when(j == 0)
    def _():
        h_ref[...] = _rms(x_ref[...], npre_ref[...]).astype(BF16)
        o_ref[...] = jnp.zeros_like(o_ref)

    h = h_ref[...]
    g = _dot(h, wg_ref[...])
    u = _dot(h, wu_ref[...])
    o_ref[...] += _dot((_silu(g) * u).astype(BF16), wd_ref[...])

    @pl.when(j == pl.num_programs(1) - 1)
    def _():
        w_post = FFN_RES * npost_ref[...]

        def rows_body(r, carry):
            rows = pl.ds(pl.multiple_of(r * NORM_ROWS, NORM_ROWS), NORM_ROWS)
            o_ref[rows, :] = x_ref[rows, :] + _rms(o_ref[rows, :], w_post)
            return carry

        lax.fori_loop(0, o_ref.shape[0] // NORM_ROWS, rows_body, 0, unroll=2)


def _ffn(x2, npre, npost, wg, wu, wd):
    t, d = x2.shape
    f = wg.shape[1]
    tm, tf = min(FFN_TOKEN_TILE, t), FF_TILE
    return pl.pallas_call(
        _ffn_kernel,
        grid=(t // tm, f // tf),
        in_specs=[
            pl.BlockSpec((tm, d), lambda i, j: (i, 0)),
            pl.BlockSpec((1, d), lambda i, j: (0, 0)),
            pl.BlockSpec((1, d), lambda i, j: (0, 0)),
            pl.BlockSpec((d, tf), lambda i, j: (0, j)),
            pl.BlockSpec((d, tf), lambda i, j: (0, j)),
            pl.BlockSpec((tf, d), lambda i, j: (j, 0)),
        ],
        out_specs=pl.BlockSpec((tm, d), lambda i, j: (i, 0)),
        out_shape=jax.ShapeDtypeStruct((t, d), F32),
        scratch_shapes=[pltpu.VMEM((tm, d), BF16)],
        compiler_params=_cparams(("parallel", "arbitrary"), FFN_VMEM_LIMIT_BYTES),
        name="ffn",
    )(x2, npre.reshape(1, d), npost.reshape(1, d), wg, wu, wd)


def _ab_proj_kernel(x_ref, nw_ref, w_ref, whf_ref, wdt_ref, o_ref, oaux_ref, h_ref):
    @pl.when(pl.program_id(1) == 0)
    def _():
        h = _rms(x_ref[...], nw_ref[...]).astype(BF16)
        h_ref[...] = h
        oaux_ref[:, :HG_WIDTH] = _dot(h, whf_ref[...])
        oaux_ref[:, HG_WIDTH:] = _dot(h, wdt_ref[...])

    o_ref[...] = _dot(h_ref[...], w_ref[...]).astype(o_ref.dtype)


def _ab_proj(x2, nw, w_in, w_dt):
    t, d = x2.shape
    tm, tn = min(PROJ_TOKEN_TILE, t), PROJ_COL_TILE
    assert tn == HG_WIDTH
    n_main = AB_MAIN_COLS // tn
    return pl.pallas_call(
        _ab_proj_kernel,
        grid=(t // tm, n_main),
        in_specs=[
            pl.BlockSpec((tm, d), lambda i, j: (i, 0)),
            pl.BlockSpec((1, d), lambda i, j: (0, 0)),
            pl.BlockSpec((d, tn), lambda i, j: (0, jnp.where(j >= 1, j + 1, j))),
            pl.BlockSpec((d, tn), lambda i, j: (0, 1)),
            pl.BlockSpec((d, LANES), lambda i, j: (0, 0)),
        ],
        out_specs=[pl.BlockSpec((tm, tn), lambda i, j: (i, j)),
                   pl.BlockSpec((tm, AB_AUX_COLS), lambda i, j: (i, 0))],
        out_shape=[jax.ShapeDtypeStruct((t, AB_MAIN_COLS), BF16), jax.ShapeDtypeStruct((t, AB_AUX_COLS), F32)],
        scratch_shapes=[pltpu.VMEM((tm, d), BF16)],
        compiler_params=_cparams(("parallel", "arbitrary")),
        name="ab_proj",
    )(x2, nw.reshape(1, d), w_in, w_in, w_dt)


def _outproj_kernel(*refs, n_parts):
    parts = refs[:n_parts]
    ws = refs[n_parts:2 * n_parts]
    npost_ref, x_ref, o_ref = refs[2 * n_parts:]
    y = _dot(parts[0][...], ws[0][...])
    for p_ref, w_ref in zip(parts[1:], ws[1:]):
        y = y + _dot(p_ref[...], w_ref[...])
    o_ref[...] = x_ref[...] + _rms(y, npost_ref[...])


def _outproj_res(parts, w, npost, x2):
    t, d = x2.shape
    tm = min(TOKEN_TILE, t)
    n_parts = len(parts)
    kp = w.shape[0] // n_parts
    in_specs = [pl.BlockSpec((tm, kp), lambda i: (i, 0)) for _ in parts]
    in_specs += [pl.BlockSpec((kp, d), functools.partial(lambda i, p: (p, 0), p=p)) for p in range(n_parts)]
    in_specs += [pl.BlockSpec((1, d), lambda i: (0, 0)), pl.BlockSpec((tm, d), lambda i: (i, 0))]
    return pl.pallas_call(
        functools.partial(_outproj_kernel, n_parts=n_parts),
        grid=(t // tm,),
        in_specs=in_specs,
        out_specs=pl.BlockSpec((tm, d), lambda i: (i, 0)),
        out_shape=jax.ShapeDtypeStruct((t, d), F32),
        compiler_params=_cparams(("parallel",)),
        name="outproj_res",
    )(*parts, *([w] * n_parts), npost.reshape(1, d), x2)


def _hgrn_kernel(q_ref, f_ref, i_ref, g_ref, lb_ref, nw_ref, o_ref, st_ref, *, layer_idx, n_chunks):
    c_len = HG_CHUNK

    @pl.when(pl.program_id(1) == 0)
    def _():
        st_ref[...] = jnp.zeros_like(st_ref)

    lbs = lb_ref[...]
    ex = jnp.exp(lbs - jnp.max(lbs, axis=0, keepdims=True))
    sm = ex / jnp.sum(ex, axis=0, keepdims=True)
    lb = jnp.sum(sm[:layer_idx + 1], axis=0, keepdims=True)
    nw = nw_ref[...]

    row = lax.broadcasted_iota(jnp.int32, (c_len, c_len), 0)
    col = lax.broadcasted_iota(jnp.int32, (c_len, c_len), 1)
    causal = col <= row
    tril = jnp.where(causal, 1.0, 0.0).astype(BF16)

    def chunk(c, carry):
        r0 = pl.multiple_of(c * c_len, c_len)
        rows = pl.ds(r0, c_len)
        f = lb + (1.0 - lb) * jax.nn.sigmoid(f_ref[0, rows, :])
        log_f = jnp.log(f)
        k = 1.0 - f
        b = _sel_dot(tril, log_f)
        b_mid = b[c_len // 2:c_len // 2 + 1, :]
        b_last = b[c_len - 1:c_len, :]
        q = q_ref[0, rows, :].astype(F32)
        v = i_ref[0, rows, :]
        q_intra = (q * jnp.exp(b - b_mid)).astype(BF16)
        k_intra = (k * jnp.exp(b_mid - b)).astype(BF16)
        q_inter = (q * jnp.exp(b)).astype(BF16)
        k_state = (k * jnp.exp(b_last - b)).astype(BF16)
        decay = jnp.exp(b_last)
        outs = []
        for h in range(HG_HEADS):
            sl = slice(h * HG_DK, (h + 1) * HG_DK)
            scores = jnp.where(causal, _dot_nt(q_intra[:, sl], k_intra[:, sl]), 0.0)
            state_t = st_ref[h]
            o = _dot(scores.astype(BF16), v[:, sl]) + _dot_nt(q_inter[:, sl], state_t.astype(BF16))
            st_ref[h] = state_t * decay[:, sl] + _dot_tn(v[:, sl], k_state[:, sl])
            ms = jnp.mean(o * o, axis=-1, keepdims=True)
            outs.append(o * lax.rsqrt(ms + EPS))
        o_all = jnp.concatenate(outs, axis=-1) * nw * _silu(g_ref[0, rows, :].astype(F32))
        o_ref[0, rows, :] = o_all.astype(o_ref.dtype)
        return carry

    lax.fori_loop(0, n_chunks, chunk, 0)


def _hgrn2(main3, aux3, hgrn_lb, norm_w, layer_idx):
    bsz, s, _ = main3.shape
    rows = min(MIXER_ROWS, s)
    w = HG_WIDTH
    blk = lambda cb: pl.BlockSpec((1, rows, w), functools.partial(lambda b, n, cb: (b, n, cb), cb=cb))
    n_lb = hgrn_lb.shape[0]
    return pl.pallas_call(
        functools.partial(_hgrn_kernel, layer_idx=layer_idx, n_chunks=rows // HG_CHUNK),
        grid=(bsz, s // rows),
        in_specs=[
            blk(0),
            blk(0),
            blk(1),
            blk(2),
            pl.BlockSpec((n_lb, w), lambda b, n: (0, 0)),
            pl.BlockSpec((1, w), lambda b, n: (0, 0)),
        ],
        out_specs=pl.BlockSpec((1, rows, w), lambda b, n: (b, n, 0)),
        out_shape=jax.ShapeDtypeStruct((bsz, s, w), BF16),
        scratch_shapes=[pltpu.VMEM((HG_HEADS, HG_DK, HG_DK), F32)],
        compiler_params=_cparams(("parallel", "arbitrary")),
        name="hgrn2",
    )(main3, aux3, main3, main3, hgrn_lb, norm_w.reshape(1, w))


def _softplus(x):
    return jnp.maximum(x, 0.0) + jnp.log1p(jnp.exp(-jnp.abs(x)))


def _ssd_kernel(z_ref, xbc_ref, dt_ref, cw_ref, cb_ref, dtb_ref, alog_ref, d_ref, nw_ref, e_ref,
                o_ref, xp_ref, ht_ref, *, n_chunks):
    c_len, n_st, p_dim = SSM_CHUNK, SSM_STATE, SSM_HEAD_DIM
    halo = SUBLANES
    gw = SSM_HPG * p_dim
    rows_total = n_chunks * c_len

    @pl.when(pl.program_id(1) == 0)
    def _():
        xp_ref[0:halo, :] = jnp.zeros((halo, SSM_CONV_CH), F32)
        ht_ref[...] = jnp.zeros_like(ht_ref)

    @pl.when(pl.program_id(1) > 0)
    def _():
        xp_ref[0:halo, :] = xp_ref[rows_total:rows_total + halo, :]

    xp_ref[halo:, :] = xbc_ref[0].astype(F32)

    row = lax.broadcasted_iota(jnp.int32, (c_len, c_len), 0)
    col = lax.broadcasted_iota(jnp.int32, (c_len, c_len), 1)
    causal = col <= row
    tril = jnp.where(causal, 1.0, 0.0).astype(BF16)
    triu = jnp.where(row <= col, 1.0, 0.0).astype(BF16)
    lane = lax.broadcasted_iota(jnp.int32, (c_len, LANES), 1)
    first_half = lane < p_dim
    expand = e_ref[...]
    neg_a = -jnp.exp(alog_ref[...])
    cw = cw_ref[...]
    cb = cb_ref[...]

    def chunk(c, carry):
        r0 = pl.multiple_of(c * c_len, c_len)
        rows = pl.ds(r0, c_len)
        full = xp_ref[pl.ds(r0, c_len + halo), :]
        conv = cb
        for j in range(SSM_CONV):
            off = halo - (SSM_CONV - 1) + j
            conv = conv + cw[j:j + 1, :] * full[off:off + c_len, :]
        act = _silu(conv)
        xs = act[:, :SSM_INNER]
        b_all = act[:, SSM_INNER:SSM_INNER + SSM_GROUPS * n_st].astype(BF16)
        c_all = act[:, SSM_INNER + SSM_GROUPS * n_st:].astype(BF16)

        dt_h = _softplus(dt_ref[0, rows, :] + dtb_ref[...])
        dta_h = dt_h * neg_a
        acs_h = _sel_dot(tril, dta_h)
        acs_t = _dot_tn_sel(dta_h, triu)
        dt_x = _dot_sel(dt_h, expand)
        acs_x = _dot_sel(acs_h, expand)
        a_last = acs_x[c_len - 1:c_len, :]
        xdt = xs * dt_x
        dec_x = (jnp.exp(a_last - acs_x) * xdt).astype(BF16)
        e_acs = jnp.exp(acs_x)
        chunk_decay = jnp.exp(a_last)

        ys = []
        for g in range(SSM_GROUPS):
            bg = b_all[:, g * n_st:(g + 1) * n_st]
            cg = c_all[:, g * n_st:(g + 1) * n_st]
            cbm = _dot_nt(cg, bg)
            y_pairs = []
            for pair in range(SSM_HPG // 2):
                lo = g * gw + pair * LANES
                x_pair = xdt[:, lo:lo + LANES]
                halves = (jnp.where(first_half, x_pair, 0.0).astype(BF16),
                          jnp.where(first_half, 0.0, x_pair).astype(BF16))
                y_pair = None
                for sub in range(2):
                    hd = g * SSM_HPG + pair * 2 + sub
                    seg = acs_h[:, hd:hd + 1] - acs_t[hd:hd + 1, :]
                    decay_ls = jnp.exp(jnp.where(causal, seg, -jnp.inf))
                    term = _dot((cbm * decay_ls).astype(BF16), halves[sub])
                    y_pair = term if y_pair is None else y_pair + term
                y_pairs.append(y_pair)
            y_diag = jnp.concatenate(y_pairs, axis=-1)
            gs = slice(g * gw, (g + 1) * gw)
            h_t = ht_ref[g]
            y_off = _dot(cg, h_t.astype(BF16)) * e_acs[:, gs]
            ht_ref[g] = h_t * chunk_decay[:, gs] + _dot_tn(bg, dec_x[:, gs])
            ys.append(y_diag + y_off)
        y = jnp.concatenate(ys, axis=-1) + xs * d_ref[...]
        y = y * _silu(z_ref[0, rows, :].astype(F32))
        normed = []
        for g in range(SSM_GROUPS):
            yg = y[:, g * gw:(g + 1) * gw]
            ms = jnp.mean(yg * yg, axis=-1, keepdims=True)
            normed.append(yg * lax.rsqrt(ms + EPS))
        o_ref[0, rows, :] = (jnp.concatenate(normed, axis=-1) * nw_ref[...]).astype(o_ref.dtype)
        return carry

    lax.fori_loop(0, n_chunks, chunk, 0)


def _ssd(main3, aux3, conv_w, conv_b, dt_bias, a_log, d_skip, norm_w):
    bsz, s, _ = main3.shape
    rows = min(MIXER_ROWS, s)
    inner, ch = SSM_INNER, SSM_CONV_CH
    pad = LANES - SSM_HEADS
    dtb = jnp.pad(dt_bias, (0, pad)).reshape(1, LANES)
    alog = jnp.pad(a_log, (0, pad)).reshape(1, LANES)
    d_x = jnp.repeat(d_skip, SSM_HEAD_DIM).reshape(1, inner)
    head_of_lane = jnp.arange(inner) // SSM_HEAD_DIM
    expand = (jnp.arange(LANES)[:, None] == head_of_lane[None, :]).astype(BF16)
    const = lambda shape: pl.BlockSpec(shape, lambda b, n: (0, 0))
    return pl.pallas_call(
        functools.partial(_ssd_kernel, n_chunks=rows // SSM_CHUNK),
        grid=(bsz, s // rows),
        in_specs=[
            pl.BlockSpec((1, rows, inner), lambda b, n: (b, n, 3)),
            pl.BlockSpec((1, rows, ch), lambda b, n: (b, n, 2)),
            pl.BlockSpec((1, rows, LANES), lambda b, n: (b, n, HG_WIDTH // LANES)),
            const((SSM_CONV, ch)), const((1, ch)), const((1, LANES)), const((1, LANES)),
            const((1, inner)), const((1, inner)), const((LANES, inner)),
        ],
        out_specs=pl.BlockSpec((1, rows, inner), lambda b, n: (b, n, 0)),
        out_shape=jax.ShapeDtypeStruct((bsz, s, inner), BF16),
        scratch_shapes=[pltpu.VMEM((rows + SUBLANES, ch), F32),
                        pltpu.VMEM((SSM_GROUPS, SSM_STATE, SSM_HPG * SSM_HEAD_DIM), F32)],
        compiler_params=_cparams(("parallel", "arbitrary")),
        name="ssd",
    )(main3, main3, aux3, conv_w, conv_b.reshape(1, ch), dtb, alog, d_x, norm_w.reshape(1, inner), expand)


def _qkv_kernel(x_ref, nw_ref, w_ref, *refs, tm, tn, dilations):
    n_grp = len(dilations)
    q_refs, k_refs, v_refs = refs[:n_grp], refs[n_grp:2 * n_grp], refs[2 * n_grp:3 * n_grp]
    h_ref, acc_ref, mid_ref = refs[3 * n_grp:]
    j = pl.program_id(2)
    per = ATT_KV_WIDTH // tn
    n_slabs = tn // LANES
    d_mid, d_far = dilations[1], dilations[2]
    fan = d_far // d_mid
    rows_mid, rows_far = tm // d_mid, tm // d_far

    @pl.when(j == 0)
    def _():
        h_ref[...] = _rms(x_ref[...], nw_ref[...]).astype(BF16)

    def project_and_emit(targets, scale):
        acc = _dot(h_ref[...], w_ref[...])
        if scale != 1.0:
            acc = acc * scale
        by_d = {d: o_ref for o_ref, d in targets}
        if 1 in by_d:
            by_d[1][0] = acc.astype(BF16)
        if d_mid not in by_d and d_far not in by_d:
            return
        for c in range(n_slabs):
            acc_ref[c] = acc[:, c * LANES:(c + 1) * LANES]
        for r in range(d_mid):
            for c in range(n_slabs):
                val = acc_ref[c, pl.ds(r, rows_mid, stride=d_mid), :]
                if d_mid in by_d:
                    by_d[d_mid][r, :, c * LANES:(c + 1) * LANES] = val.astype(BF16)
                if d_far in by_d:
                    mid_ref[c, r * rows_mid:(r + 1) * rows_mid, :] = val
        if d_far in by_d:
            for r in range(d_mid):
                for k in range(fan):
                    for c in range(n_slabs):
                        val = mid_ref[c, pl.ds(r * rows_mid + k, rows_far, stride=fan), :]
                        by_d[d_far][r + d_mid * k, :, c * LANES:(c + 1) * LANES] = val.astype(BF16)

    for g, d in enumerate(dilations):
        @pl.when((j >= g * per) & (j < (g + 1) * per))
        def _(g=g, d=d):
            project_and_emit([(q_refs[g], d)], ATT_HEAD_DIM ** -0.5)

    @pl.when((j >= n_grp * per) & (j < (n_grp + 1) * per))
    def _():
        project_and_emit(list(zip(k_refs, dilations)), 1.0)

    @pl.when(j >= (n_grp + 1) * per)
    def _():
        project_and_emit(list(zip(v_refs, dilations)), 1.0)


def _qkv_proj(x3, nw, w):
    bsz, s, dm = x3.shape
    tm, tn = min(PROJ_TOKEN_TILE, s), QKV_COL_TILE
    per = ATT_KV_WIDTH // tn
    dilations = tuple(d for _, d in ATT_PATTERNS)
    n_grp = len(dilations)
    assert dilations[0] == 1 and dilations[2] % dilations[1] == 0

    def out_spec(d, j0):
        return pl.BlockSpec((None, d, tm // d, tn),
                            lambda b, i, j: (b, 0, i, jnp.clip(j - j0, 0, per - 1)))

    starts = [g * per for g in range(n_grp)] + [n_grp * per] * n_grp + [(n_grp + 1) * per] * n_grp
    out_specs = [out_spec(d, j0) for d, j0 in zip(dilations * 3, starts)]
    out_shape = [jax.ShapeDtypeStruct((bsz, d, s // d, ATT_KV_WIDTH), BF16) for d in dilations * 3]
    outs = pl.pallas_call(
        functools.partial(_qkv_kernel, tm=tm, tn=tn, dilations=dilations),
        grid=(bsz, s // tm, w.shape[1] // tn),
        in_specs=[
            pl.BlockSpec((None, tm, dm), lambda b, i, j: (b, i, 0)),
            pl.BlockSpec((1, dm), lambda b, i, j: (0, 0)),
            pl.BlockSpec((dm, tn), lambda b, i, j: (0, j)),
        ],
        out_specs=out_specs,
        out_shape=out_shape,
        scratch_shapes=[pltpu.VMEM((tm, dm), BF16), pltpu.VMEM((tn // LANES, tm, LANES), F32),
                        pltpu.VMEM((tn // LANES, tm, LANES), F32)],
        compiler_params=_cparams(("parallel", "parallel", "arbitrary")),
        name="qkv_proj",
    )(x3, nw.reshape(1, dm), w)
    return outs[:n_grp], outs[n_grp:2 * n_grp], outs[2 * n_grp:]


def _attn_kernel(q_ref, k_ref, v_ref, kp_ref, vp_ref, o_ref, st_ref, *, n_qblocks, span):
    blk, dh = ATT_BLOCK, ATT_HEAD_DIM
    n = pl.program_id(2)
    row = lax.broadcasted_iota(jnp.int32, (blk, 2 * blk), 0)
    col = lax.broadcasted_iota(jnp.int32, (blk, 2 * blk), 1)
    dist = row + blk - col
    mask = (dist >= 0) & (dist <= span)
    mask_first = mask & ((col >= blk) | (n > 0))
    lane = lax.broadcasted_iota(jnp.int32, (blk, LANES), 1)
    ones = jnp.ones((2 * blk, dh), BF16)

    for j in range(n_qblocks):
        rows = slice(j * blk, (j + 1) * blk)
        tile = jnp.zeros((blk, LANES), F32)
        for h in range(ATT_KV_HEADS):
            sl = slice(h * dh, (h + 1) * dh)
            q = q_ref[rows, sl]
            if j == 0:
                k2 = jnp.concatenate([kp_ref[:, sl], k_ref[rows, sl]], axis=0)
                v2 = jnp.concatenate([vp_ref[:, sl], v_ref[rows, sl]], axis=0)
            else:
                rows2 = slice((j - 1) * blk, (j + 1) * blk)
                k2 = k_ref[rows2, sl]
                v2 = v_ref[rows2, sl]
            s = jnp.where(mask_first if j == 0 else mask, _dot_nt(q, k2), -jnp.inf)
            m = jnp.max(s, axis=-1, keepdims=True)
            p = jnp.exp(s - m).astype(BF16)
            ov = _dot(p, jnp.concatenate([v2, ones], axis=1))
            l = ov[:, dh:]
            o_ref[rows, sl] = (ov[:, :dh] / l).astype(o_ref.dtype)
            tile = jnp.where(lane == h, m + jnp.log(l), tile)
        st_ref[rows, :] = tile


def _attn_group(q, k, v, span):
    bsz, d, m, w = q.shape
    rows = min(ATT_ROWS, m)
    qb_per_step = rows // ATT_BLOCK
    cur = pl.BlockSpec((None, None, rows, w), lambda b, r, n: (b, r, n, 0))
    prev = pl.BlockSpec((None, None, ATT_BLOCK, w), lambda b, r, n: (b, r, jnp.maximum(n * qb_per_step - 1, 0), 0))
    return pl.pallas_call(
        functools.partial(_attn_kernel, n_qblocks=qb_per_step, span=span),
        grid=(bsz, d, m // rows),
        in_specs=[cur, cur, cur, prev, prev],
        out_specs=[cur, pl.BlockSpec((None, None, rows, LANES), lambda b, r, n: (b, r, n, 0))],
        out_shape=[jax.ShapeDtypeStruct((bsz, d, m, w), BF16), jax.ShapeDtypeStruct((bsz, d, m, LANES), F32)],
        compiler_params=_cparams(("parallel", "parallel", "arbitrary")),
        name=f"attn_d{d}",
    )(q, k, v, k, v)


def _attn_out_kernel(*refs, tm, dilations):
    n_grp = len(dilations)
    o_refs, s_refs = refs[:n_grp], refs[n_grp:2 * n_grp]
    w_ref, npost_ref, x_ref, out_ref, on_ref, sn_ref, comb_ref = refs[2 * n_grp:]
    dh = ATT_HEAD_DIM

    for g, d in enumerate(dilations):
        rows = tm // d
        for r in range(d):
            if d == 1:
                sn_ref[g] = s_refs[g][r]
            else:
                sn_ref[g, pl.ds(r, rows, stride=d), :] = s_refs[g][r]
                for h in range(ATT_KV_HEADS):
                    on_ref[g, h, pl.ds(r, rows, stride=d), :] = o_refs[g][r, :, h * dh:(h + 1) * dh].astype(F32)

    lses = [sn_ref[g] for g in range(n_grp)]
    mx = functools.reduce(jnp.maximum, lses)
    es = [jnp.exp(v - mx) for v in lses]
    inv = 1.0 / functools.reduce(lambda a, b: a + b, es)
    wts = [e * inv for e in es]
    y = None
    for h in range(ATT_KV_HEADS):
        sl = slice(h * dh, (h + 1) * dh)
        acc = None
        for g, d in enumerate(dilations):
            og = o_refs[g][0, :, sl].astype(F32) if d == 1 else on_ref[g, h]
            term = wts[g][:, h:h + 1] * og
            acc = term if acc is None else acc + term
        comb_ref[:, sl] = acc.astype(BF16)
        if h % 2 == 1:
            pair = slice((h - 1) * dh, (h + 1) * dh)
            part = _dot(comb_ref[:, pair], w_ref[pair, :])
            y = part if y is None else y + part
    out_ref[...] = x_ref[...] + _rms(y, npost_ref[...])


def _attn_out(os, sts, w, npost, x3):
    bsz, s, dm = x3.shape
    tm = min(ATTN_OUT_TILE, s)
    kw = ATT_KV_WIDTH
    dilations = tuple(o.shape[1] for o in os)
    n_grp = len(dilations)
    grp_blk = lambda d, width: pl.BlockSpec((None, d, tm // d, width), lambda b, i: (b, 0, i, 0))
    row_blk = pl.BlockSpec((None, tm, dm), lambda b, i: (b, i, 0))
    return pl.pallas_call(
        functools.partial(_attn_out_kernel, tm=tm, dilations=dilations),
        grid=(bsz, s // tm),
        in_specs=[grp_blk(d, kw) for d in dilations] + [grp_blk(d, LANES) for d in dilations] + [
            pl.BlockSpec((kw, dm), lambda b, i: (0, 0)),
            pl.BlockSpec((1, dm), lambda b, i: (0, 0)),
            row_blk,
        ],
        out_specs=row_blk,
        out_shape=jax.ShapeDtypeStruct((bsz, s, dm), F32),
        scratch_shapes=[pltpu.VMEM((n_grp, ATT_KV_HEADS, tm, LANES), F32),
                        pltpu.VMEM((n_grp, tm, LANES), F32),
                        pltpu.VMEM((tm, kw), BF16)],
        compiler_params=_cparams(("parallel", "parallel")),
        name="attn_out",
    )(*os, *sts, w, npost.reshape(1, dm), x3)


def _mixer_ab(h2, bsz, s, nw_pre, nw_post, w_in, w_out, hgrn_lb, layer_idx, hg_norm_w,
              conv_w, conv_b, dt_bias, a_log, d_skip, ssm_norm_w):
    dt_col = 4 * HG_WIDTH + SSM_INNER + SSM_CONV_CH
    w_dt = jnp.pad(w_in[:, dt_col:], ((0, 0), (0, LANES - SSM_HEADS))).astype(BF16)
    main, aux = _ab_proj(h2, nw_pre, w_in.astype(BF16), w_dt)
    main3 = main.reshape(bsz, s, AB_MAIN_COLS)
    aux3 = aux.reshape(bsz, s, AB_AUX_COLS)
    o_a = _hgrn2(main3, aux3, hgrn_lb, hg_norm_w, layer_idx)
    o_b = _ssd(main3, aux3, conv_w, conv_b, dt_bias, a_log, d_skip, ssm_norm_w)
    t = bsz * s
    return _outproj_res([o_a.reshape(t, HG_WIDTH), o_b.reshape(t, SSM_INNER)], w_out.astype(BF16), nw_post, h2)


def _mixer_c(h2, bsz, s, nw_pre, nw_post, w_in, w_out):
    dm = h2.shape[1]
    x3 = h2.reshape(bsz, s, dm)
    qs, ks, vs = _qkv_proj(x3, nw_pre, w_in.astype(BF16))
    os, sts = [], []
    for g, (window, dilation) in enumerate(ATT_PATTERNS):
        o, st = _attn_group(qs[g], ks[g], vs[g], window // dilation)
        os.append(o)
        sts.append(st)
    return _attn_out(os, sts, w_out.astype(BF16), nw_post, x3).reshape(bsz * s, dm)


def kernel(x, norm_pre, norm_post, ffn_w_gate, ffn_w_up, ffn_w_down, ab_w_in, ab_w_out, hgrn_lb, hgrn_norm_w,
           ssm_conv_w, ssm_conv_b, ssm_dt_bias, ssm_A_log, ssm_D, ssm_norm_w, att_w_in, att_w_out):
    bsz, s, d = x.shape
    depth = norm_pre.shape[0]
    h2 = x.reshape(bsz * s, d)

    def ffn(h2, layer, k):
        return _ffn(h2, norm_pre[layer, 2 * k], norm_post[layer, 2 * k],
                    ffn_w_gate[layer, k].astype(BF16), ffn_w_up[layer, k].astype(BF16),
                    ffn_w_down[layer, k].astype(BF16))

    for layer in range(depth):
        h2 = ffn(h2, layer, 0)
        if layer % 2 == 0:
            e = layer // 2
            h2 = _mixer_ab(h2, bsz, s, norm_pre[layer, 1], norm_post[layer, 1], ab_w_in[e], ab_w_out[e],
                           hgrn_lb, e, hgrn_norm_w[e], ssm_conv_w[e], ssm_conv_b[e], ssm_dt_bias[e],
                           ssm_A_log[e], ssm_D[e], ssm_norm_w[e])
        else:
            o_idx = layer // 2
            h2 = _mixer_c(h2, bsz, s, norm_pre[layer, 1], norm_post[layer, 1], att_w_in[o_idx], att_w_out[o_idx])
        h2 = ffn(h2, layer, 1)
    return h2.reshape(bsz, s, d)
```

```python
import functools

import jax
import jax.numpy as jnp
from jax import lax
from jax.experimental import pallas as pl
from jax.experimental.pallas import tpu as pltpu

F32 = jnp.float32
BF16 = jnp.bfloat16
EPS = 1e-6
FFN_RES = 0.5

HG_HEADS = 8
HG_DK = 128
HG_WIDTH = HG_HEADS * HG_DK
HG_CHUNK = 64
HG_INTRA_ROWS = 128

SSM_INNER = 1024
SSM_HEAD_DIM = 64
SSM_HEADS = SSM_INNER // SSM_HEAD_DIM
SSM_GROUPS = 4
SSM_HPG = SSM_HEADS // SSM_GROUPS
SSM_STATE = 128
SSM_CONV = 4
SSM_CHUNK = 128
SSM_CONV_CH = SSM_INNER + 2 * SSM_GROUPS * SSM_STATE

ATT_PATTERNS = ((128, 1), (512, 4), (2048, 16))
ATT_GROUPS = 3
ATT_KV_HEADS = 16
ATT_HEAD_DIM = 128
ATT_BLOCK = 128
ATT_Q_WIDTH = ATT_GROUPS * ATT_KV_HEADS * ATT_HEAD_DIM
ATT_KV_WIDTH = ATT_KV_HEADS * ATT_HEAD_DIM
ATT_IN = ATT_Q_WIDTH + 2 * ATT_KV_WIDTH

LANES = 128
SUBLANES = 8
VMEM_LIMIT_BYTES = 56 * 1024 * 1024
FFN_VMEM_LIMIT_BYTES = 60 * 1024 * 1024

TOKEN_TILE = 512
FFN_TOKEN_TILE = 1024
PROJ_TOKEN_TILE = 1024
QKV_COL_TILE = 512
NORM_ROWS = 128
FF_TILE = 512
PROJ_COL_TILE = 1024
MIXER_ROWS = 512
ATT_ROWS = 512
ATTN_OUT_TILE = 256

AB_MAIN_COLS = 4 * HG_WIDTH + SSM_CONV_CH
AB_AUX_COLS = HG_WIDTH + LANES


def _cparams(semantics, vmem_limit_bytes=VMEM_LIMIT_BYTES):
    return pltpu.CompilerParams(dimension_semantics=semantics, vmem_limit_bytes=vmem_limit_bytes)


def _rms(x, w):
    ms = jnp.mean(x * x, axis=-1, keepdims=True)
    return x * lax.rsqrt(ms + EPS) * w


def _silu(x):
    return x * jax.nn.sigmoid(x)


def _dot(a, b):
    return jnp.dot(a, b, preferred_element_type=F32)


def _dot_nt(a, b):
    return lax.dot_general(a, b, (((1,), (1,)), ((), ())), preferred_element_type=F32)


def _dot_tn(a, b):
    return lax.dot_general(a, b, (((0,), (0,)), ((), ())), preferred_element_type=F32)


def _split3(x):
    hi = x.astype(BF16)
    r = x - hi.astype(F32)
    mid = r.astype(BF16)
    lo = (r - mid.astype(F32)).astype(BF16)
    return hi, mid, lo


def _sel_dot(sel, x):
    hi, mid, lo = _split3(x)
    return _dot(sel, hi) + _dot(sel, mid) + _dot(sel, lo)


def _dot_sel(x, sel):
    hi, mid, lo = _split3(x)
    return _dot(hi, sel) + _dot(mid, sel) + _dot(lo, sel)


def _dot_tn_sel(x, sel):
    hi, mid, lo = _split3(x)
    return _dot_tn(hi, sel) + _dot_tn(mid, sel) + _dot_tn(lo, sel)


def _ffn_kernel(x_ref, npre_ref, npost_ref, wg_ref, wu_ref, wd_ref, o_ref, h_ref):
    j = pl.program_id(1)

    @pl.when(j == 0)
    def _():
        h_ref[...] = _rms(x_ref[...], npre_ref[...]).astype(BF16)
        o_ref[...] = jnp.zeros_like(o_ref)

    h = h_ref[...]
    g = _dot(h, wg_ref[...])
    u = _dot(h, wu_ref[...])
    o_ref[...] += _dot((_silu(g) * u).astype(BF16), wd_ref[...])

    @pl.when(j == pl.num_programs(1) - 1)
    def _():
        w_post = FFN_RES * npost_ref[...]

        def rows_body(r, carry):
            rows = pl.ds(pl.multiple_of(r * NORM_ROWS, NORM_ROWS), NORM_ROWS)
            o_ref[rows, :] = x_ref[rows, :] + _rms(o_ref[rows, :], w_post)
            return carry

        lax.fori_loop(0, o_ref.shape[0] // NORM_ROWS, rows_body, 0, unroll=2)


def _ffn(x2, npre, npost, wg, wu, wd, layer, k):
    t, d = x2.shape
    f = wg.shape[-1]
    tm, tf = min(FFN_TOKEN_TILE, t), FF_TILE
    return pl.pallas_call(
        _ffn_kernel,
        grid=(t // tm, f // tf),
        in_specs=[
            pl.BlockSpec((tm, d), lambda i, j: (i, 0)),
            pl.BlockSpec((1, d), lambda i, j: (0, 0)),
            pl.BlockSpec((1, d), lambda i, j: (0, 0)),
            pl.BlockSpec((None, None, d, tf), lambda i, j: (layer, k, 0, j)),
            pl.BlockSpec((None, None, d, tf), lambda i, j: (layer, k, 0, j)),
            pl.BlockSpec((None, None, tf, d), lambda i, j: (layer, k, j, 0)),
        ],
        out_specs=pl.BlockSpec((tm, d), lambda i, j: (i, 0)),
        out_shape=jax.ShapeDtypeStruct((t, d), F32),
        scratch_shapes=[pltpu.VMEM((tm, d), BF16)],
        compiler_params=_cparams(("parallel", "arbitrary"), FFN_VMEM_LIMIT_BYTES),
        name="ffn",
    )(x2, npre.reshape(1, d), npost.reshape(1, d), wg, wu, wd)


def _ab_proj_kernel(x_ref, nw_ref, w_ref, whf_ref, wdt_ref, o_ref, oaux_ref, h_ref):
    @pl.when(pl.program_id(1) == 0)
    def _():
        h = _rms(x_ref[...], nw_ref[...]).astype(BF16)
        h_ref[...] = h
        oaux_ref[:, :HG_WIDTH] = _dot(h, whf_ref[...])
        oaux_ref[:, HG_WIDTH:] = _dot(h, wdt_ref[...])

    o_ref[...] = _dot(h_ref[...], w_ref[...]).astype(o_ref.dtype)


def _ab_proj(x2, nw, w_in, w_dt):
    t, d = x2.shape
    tm, tn = min(PROJ_TOKEN_TILE, t), PROJ_COL_TILE
    assert tn == HG_WIDTH
    n_main = AB_MAIN_COLS // tn
    return pl.pallas_call(
        _ab_proj_kernel,
        grid=(t // tm, n_main),
        in_specs=[
            pl.BlockSpec((tm, d), lambda i, j: (i, 0)),
            pl.BlockSpec((1, d), lambda i, j: (0, 0)),
            pl.BlockSpec((d, tn), lambda i, j: (0, jnp.where(j >= 1, j + 1, j))),
            pl.BlockSpec((d, tn), lambda i, j: (0, 1)),
            pl.BlockSpec((d, LANES), lambda i, j: (0, 0)),
        ],
        out_specs=[pl.BlockSpec((tm, tn), lambda i, j: (i, j)),
                   pl.BlockSpec((tm, AB_AUX_COLS), lambda i, j: (i, 0))],
        out_shape=[jax.ShapeDtypeStruct((t, AB_MAIN_COLS), BF16), jax.ShapeDtypeStruct((t, AB_AUX_COLS), F32)],
        scratch_shapes=[pltpu.VMEM((tm, d), BF16)],
        compiler_params=_cparams(("parallel", "arbitrary")),
        name="ab_proj",
    )(x2, nw.reshape(1, d), w_in, w_in, w_dt)


def _outproj_kernel(*refs, n_parts):
    parts = refs[:n_parts]
    ws = refs[n_parts:2 * n_parts]
    npost_ref, x_ref, o_ref = refs[2 * n_parts:]
    y = _dot(parts[0][...], ws[0][...])
    for p_ref, w_ref in zip(parts[1:], ws[1:]):
        y = y + _dot(p_ref[...], w_ref[...])
    o_ref[...] = x_ref[...] + _rms(y, npost_ref[...])


def _outproj_res(parts, w, npost, x2):
    t, d = x2.shape
    tm = min(TOKEN_TILE, t)
    n_parts = len(parts)
    kp = w.shape[0] // n_parts
    in_specs = [pl.BlockSpec((tm, kp), lambda i: (i, 0)) for _ in parts]
    in_specs += [pl.BlockSpec((kp, d), functools.partial(lambda i, p: (p, 0), p=p)) for p in range(n_parts)]
    in_specs += [pl.BlockSpec((1, d), lambda i: (0, 0)), pl.BlockSpec((tm, d), lambda i: (i, 0))]
    return pl.pallas_call(
        functools.partial(_outproj_kernel, n_parts=n_parts),
        grid=(t // tm,),
        in_specs=in_specs,
        out_specs=pl.BlockSpec((tm, d), lambda i: (i, 0)),
        out_shape=jax.ShapeDtypeStruct((t, d), F32),
        compiler_params=_cparams(("parallel",)),
        name="outproj_res",
    )(*parts, *([w] * n_parts), npost.reshape(1, d), x2)


def _hgrn_kernel(q_ref, f_ref, i_ref, g_ref, lb_ref, nw_ref, o_ref, st_ref, *, layer_idx, n_chunks):
    c_len = HG_CHUNK

    @pl.when(pl.program_id(1) == 0)
    def _():
        st_ref[...] = jnp.zeros_like(st_ref)

    lbs = lb_ref[...]
    ex = jnp.exp(lbs - jnp.max(lbs, axis=0, keepdims=True))
    sm = ex / jnp.sum(ex, axis=0, keepdims=True)
    lb = jnp.sum(sm[:layer_idx + 1], axis=0, keepdims=True)

    span = HG_INTRA_ROWS
    rows_total = n_chunks * c_len
    per_span = span // c_len
    row = lax.broadcasted_iota(jnp.int32, (span, span), 0)
    col = lax.broadcasted_iota(jnp.int32, (span, span), 1)
    same_chunk = (row // c_len) == (col // c_len)
    causal = (col <= row) & same_chunk
    tril = jnp.where(causal, 1.0, 0.0).astype(BF16)

    q_intra, k_intra, q_inter, k_state, decays = [], [], [], [], []
    for s0 in range(0, rows_total, span):
        rows = slice(s0, s0 + span)
        f = lb + (1.0 - lb) * jax.nn.sigmoid(f_ref[0, rows, :])
        log_f = jnp.log(f)
        k = 1.0 - f
        cum = _sel_dot(tril, log_f)
        cum3 = cum.reshape(per_span, c_len, HG_WIDTH)
        b_mid = jnp.broadcast_to(cum3[:, c_len // 2:c_len // 2 + 1, :], cum3.shape).reshape(span, HG_WIDTH)
        last = cum3[:, c_len - 1:c_len, :]
        b_last = jnp.broadcast_to(last, cum3.shape).reshape(span, HG_WIDTH)
        q = q_ref[0, rows, :].astype(F32)
        q_intra.append((q * jnp.exp(cum - b_mid)).astype(BF16))
        k_intra.append((k * jnp.exp(b_mid - cum)).astype(BF16))
        q_inter.append((q * jnp.exp(cum)).astype(BF16))
        k_state.append((k * jnp.exp(b_last - cum)).astype(BF16))
        decays.extend(jnp.exp(last[c]) for c in range(per_span))

    for h in range(HG_HEADS):
        sl = slice(h * HG_DK, (h + 1) * HG_DK)
        state_t = st_ref[h]
        outs = []
        for si, s0 in enumerate(range(0, rows_total, span)):
            v = i_ref[0, s0:s0 + span, sl]
            scores = jnp.where(causal, _dot_nt(q_intra[si][:, sl], k_intra[si][:, sl]), 0.0)
            o_intra = _dot(scores.astype(BF16), v)
            inter = []
            for c in range(per_span):
                cr = slice(c * c_len, (c + 1) * c_len)
                inter.append(_dot_nt(q_inter[si][cr, sl], state_t.astype(BF16)))
                state_t = state_t * decays[si * per_span + c][:, sl] + _dot_tn(v[cr, :], k_state[si][cr, sl])
            outs.append(o_intra + jnp.concatenate(inter, axis=0))
        st_ref[h] = state_t
        o = jnp.concatenate(outs, axis=0)
        ms = jnp.mean(o * o, axis=-1, keepdims=True)
        gate = _silu(g_ref[0, :, sl].astype(F32))
        o_ref[0, :, sl] = (o * lax.rsqrt(ms + EPS) * nw_ref[:, sl] * gate).astype(o_ref.dtype)


def _hgrn2(main3, aux3, hgrn_lb, norm_w, layer_idx):
    bsz, s, _ = main3.shape
    rows = min(MIXER_ROWS, s)
    w = HG_WIDTH
    blk = lambda cb: pl.BlockSpec((1, rows, w), functools.partial(lambda b, n, cb: (b, n, cb), cb=cb))
    n_lb = hgrn_lb.shape[0]
    return pl.pallas_call(
        functools.partial(_hgrn_kernel, layer_idx=layer_idx, n_chunks=rows // HG_CHUNK),
        grid=(bsz, s // rows),
        in_specs=[
            blk(0),
            blk(0),
            blk(1),
            blk(2),
            pl.BlockSpec((n_lb, w), lambda b, n: (0, 0)),
            pl.BlockSpec((1, w), lambda b, n: (0, 0)),
        ],
        out_specs=pl.BlockSpec((1, rows, w), lambda b, n: (b, n, 0)),
        out_shape=jax.ShapeDtypeStruct((bsz, s, w), BF16),
        scratch_shapes=[pltpu.VMEM((HG_HEADS, HG_DK, HG_DK), F32)],
        compiler_params=_cparams(("parallel", "arbitrary")),
        name="hgrn2",
    )(main3, aux3, main3, main3, hgrn_lb, norm_w.reshape(1, w))


def _softplus(x):
    return jnp.maximum(x, 0.0) + jnp.log1p(jnp.exp(-jnp.abs(x)))


def _ssd_kernel(z_ref, xbc_ref, dt_ref, cw_ref, cb_ref, dtb_ref, alog_ref, d_ref, nw_ref, e_ref,
                o_ref, xp_ref, ht_ref, *, n_chunks):
    c_len, n_st, p_dim = SSM_CHUNK, SSM_STATE, SSM_HEAD_DIM
    halo = SUBLANES
    gw = SSM_HPG * p_dim
    rows_total = n_chunks * c_len

    @pl.when(pl.program_id(1) == 0)
    def _():
        xp_ref[0:halo, :] = jnp.zeros((halo, SSM_CONV_CH), F32)
        ht_ref[...] = jnp.zeros_like(ht_ref)

    @pl.when(pl.program_id(1) > 0)
    def _():
        xp_ref[0:halo, :] = xp_ref[rows_total:rows_total + halo, :]

    xp_ref[halo:, :] = xbc_ref[0].astype(F32)

    row = lax.broadcasted_iota(jnp.int32, (c_len, c_len), 0)
    col = lax.broadcasted_iota(jnp.int32, (c_len, c_len), 1)
    causal = col <= row
    tril = jnp.where(causal, 1.0, 0.0).astype(BF16)
    triu = jnp.where(row <= col, 1.0, 0.0).astype(BF16)
    lane = lax.broadcasted_iota(jnp.int32, (c_len, LANES), 1)
    first_half = lane < p_dim
    expand = e_ref[...]
    neg_a = -jnp.exp(alog_ref[...])
    cw = cw_ref[...]
    cb = cb_ref[...]

    def chunk(c, carry):
        r0 = pl.multiple_of(c * c_len, c_len)
        rows = pl.ds(r0, c_len)
        full = xp_ref[pl.ds(r0, c_len + halo), :]
        conv = cb
        for j in range(SSM_CONV):
            off = halo - (SSM_CONV - 1) + j
            conv = conv + cw[j:j + 1, :] * full[off:off + c_len, :]
        act = _silu(conv)
        xs = act[:, :SSM_INNER]
        b_all = act[:, SSM_INNER:SSM_INNER + SSM_GROUPS * n_st].astype(BF16)
        c_all = act[:, SSM_INNER + SSM_GROUPS * n_st:].astype(BF16)

        dt_h = _softplus(dt_ref[0, rows, :] + dtb_ref[...])
        dta_h = dt_h * neg_a
        acs_h = _sel_dot(tril, dta_h)
        acs_t = _dot_tn_sel(dta_h, triu)
        dt_x = _dot_sel(dt_h, expand)
        acs_x = _dot_sel(acs_h, expand)
        a_last = acs_x[c_len - 1:c_len, :]
        xdt = xs * dt_x
        dec_x = (jnp.exp(a_last - acs_x) * xdt).astype(BF16)
        e_acs = jnp.exp(acs_x)
        chunk_decay = jnp.exp(a_last)

        ys = []
        for g in range(SSM_GROUPS):
            bg = b_all[:, g * n_st:(g + 1) * n_st]
            cg = c_all[:, g * n_st:(g + 1) * n_st]
            cbm = _dot_nt(cg, bg)
            y_pairs = []
            for pair in range(SSM_HPG // 2):
                lo = g * gw + pair * LANES
                x_pair = xdt[:, lo:lo + LANES]
                halves = (jnp.where(first_half, x_pair, 0.0).astype(BF16),
                          jnp.where(first_half, 0.0, x_pair).astype(BF16))
                y_pair = None
                for sub in range(2):
                    hd = g * SSM_HPG + pair * 2 + sub
                    seg = acs_h[:, hd:hd + 1] - acs_t[hd:hd + 1, :]
                    decay_ls = jnp.exp(jnp.where(causal, seg, -jnp.inf))
                    term = _dot((cbm * decay_ls).astype(BF16), halves[sub])
                    y_pair = term if y_pair is None else y_pair + term
                y_pairs.append(y_pair)
            y_diag = jnp.concatenate(y_pairs, axis=-1)
            gs = slice(g * gw, (g + 1) * gw)
            h_t = ht_ref[g]
            y_off = _dot(cg, h_t.astype(BF16)) * e_acs[:, gs]
            ht_ref[g] = h_t * chunk_decay[:, gs] + _dot_tn(bg, dec_x[:, gs])
            ys.append(y_diag + y_off)
        y = jnp.concatenate(ys, axis=-1) + xs * d_ref[...]
        y = y * _silu(z_ref[0, rows, :].astype(F32))
        normed = []
        for g in range(SSM_GROUPS):
            yg = y[:, g * gw:(g + 1) * gw]
            ms = jnp.mean(yg * yg, axis=-1, keepdims=True)
            normed.append(yg * lax.rsqrt(ms + EPS))
        o_ref[0, rows, :] = (jnp.concatenate(normed, axis=-1) * nw_ref[...]).astype(o_ref.dtype)
        return carry

    lax.fori_loop(0, n_chunks, chunk, 0)


def _ssd(main3, aux3, conv_w, conv_b, dt_bias, a_log, d_skip, norm_w):
    bsz, s, _ = main3.shape
    rows = min(MIXER_ROWS, s)
    inner, ch = SSM_INNER, SSM_CONV_CH
    pad = LANES - SSM_HEADS
    dtb = jnp.pad(dt_bias, (0, pad)).reshape(1, LANES)
    alog = jnp.pad(a_log, (0, pad)).reshape(1, LANES)
    d_x = jnp.repeat(d_skip, SSM_HEAD_DIM).reshape(1, inner)
    head_of_lane = jnp.arange(inner) // SSM_HEAD_DIM
    expand = (jnp.arange(LANES)[:, None] == head_of_lane[None, :]).astype(BF16)
    const = lambda shape: pl.BlockSpec(shape, lambda b, n: (0, 0))
    return pl.pallas_call(
        functools.partial(_ssd_kernel, n_chunks=rows // SSM_CHUNK),
        grid=(bsz, s // rows),
        in_specs=[
            pl.BlockSpec((1, rows, inner), lambda b, n: (b, n, 3)),
            pl.BlockSpec((1, rows, ch), lambda b, n: (b, n, 2)),
            pl.BlockSpec((1, rows, LANES), lambda b, n: (b, n, HG_WIDTH // LANES)),
            const((SSM_CONV, ch)), const((1, ch)), const((1, LANES)), const((1, LANES)),
            const((1, inner)), const((1, inner)), const((LANES, inner)),
        ],
        out_specs=pl.BlockSpec((1, rows, inner), lambda b, n: (b, n, 0)),
        out_shape=jax.ShapeDtypeStruct((bsz, s, inner), BF16),
        scratch_shapes=[pltpu.VMEM((rows + SUBLANES, ch), F32),
                        pltpu.VMEM((SSM_GROUPS, SSM_STATE, SSM_HPG * SSM_HEAD_DIM), F32)],
        compiler_params=_cparams(("parallel", "arbitrary")),
        name="ssd",
    )(main3, main3, aux3, conv_w, conv_b.reshape(1, ch), dtb, alog, d_x, norm_w.reshape(1, inner), expand)


def _qkv_kernel(x_ref, nw_ref, w_ref, *refs, tm, tn, dilations):
    n_grp = len(dilations)
    q_refs, k_refs, v_refs = refs[:n_grp], refs[n_grp:2 * n_grp], refs[2 * n_grp:3 * n_grp]
    h_ref, acc_ref, mid_ref = refs[3 * n_grp:]
    j = pl.program_id(2)
    per = ATT_KV_WIDTH // tn
    n_slabs = tn // LANES
    d_mid, d_far = dilations[1], dilations[2]
    fan = d_far // d_mid
    rows_mid, rows_far = tm // d_mid, tm // d_far

    @pl.when(j == 0)
    def _():
        h_ref[...] = _rms(x_ref[...], nw_ref[...]).astype(BF16)

    def project_and_emit(targets, scale):
        acc = _dot(h_ref[...], w_ref[...])
        if scale != 1.0:
            acc = acc * scale
        by_d = {d: o_ref for o_ref, d in targets}
        if 1 in by_d:
            by_d[1][0] = acc.astype(BF16)
        if d_mid not in by_d and d_far not in by_d:
            return
        for c in range(n_slabs):
            acc_ref[c] = acc[:, c * LANES:(c + 1) * LANES]
        for r in range(d_mid):
            for c in range(n_slabs):
                val = acc_ref[c, pl.ds(r, rows_mid, stride=d_mid), :]
                if d_mid in by_d:
                    by_d[d_mid][r, :, c * LANES:(c + 1) * LANES] = val.astype(BF16)
                if d_far in by_d:
                    mid_ref[c, r * rows_mid:(r + 1) * rows_mid, :] = val
        if d_far in by_d:
            for r in range(d_mid):
                for k in range(fan):
                    for c in range(n_slabs):
                        val = mid_ref[c, pl.ds(r * rows_mid + k, rows_far, stride=fan), :]
                        by_d[d_far][r + d_mid * k, :, c * LANES:(c + 1) * LANES] = val.astype(BF16)

    for g, d in enumerate(dilations):
        @pl.when((j >= g * per) & (j < (g + 1) * per))
        def _(g=g, d=d):
            project_and_emit([(q_refs[g], d)], ATT_HEAD_DIM ** -0.5)

    @pl.when((j >= n_grp * per) & (j < (n_grp + 1) * per))
    def _():
        project_and_emit(list(zip(k_refs, dilations)), 1.0)

    @pl.when(j >= (n_grp + 1) * per)
    def _():
        project_and_emit(list(zip(v_refs, dilations)), 1.0)


def _qkv_proj(x3, nw, w):
    bsz, s, dm = x3.shape
    tm, tn = min(PROJ_TOKEN_TILE, s), QKV_COL_TILE
    per = ATT_KV_WIDTH // tn
    dilations = tuple(d for _, d in ATT_PATTERNS)
    n_grp = len(dilations)
    assert dilations[0] == 1 and dilations[2] % dilations[1] == 0

    def out_spec(d, j0):
        return pl.BlockSpec((None, d, tm // d, tn),
                            lambda b, i, j: (b, 0, i, jnp.clip(j - j0, 0, per - 1)))

    starts = [g * per for g in range(n_grp)] + [n_grp * per] * n_grp + [(n_grp + 1) * per] * n_grp
    out_specs = [out_spec(d, j0) for d, j0 in zip(dilations * 3, starts)]
    out_shape = [jax.ShapeDtypeStruct((bsz, d, s // d, ATT_KV_WIDTH), BF16) for d in dilations * 3]
    outs = pl.pallas_call(
        functools.partial(_qkv_kernel, tm=tm, tn=tn, dilations=dilations),
        grid=(bsz, s // tm, w.shape[1] // tn),
        in_specs=[
            pl.BlockSpec((None, tm, dm), lambda b, i, j: (b, i, 0)),
            pl.BlockSpec((1, dm), lambda b, i, j: (0, 0)),
            pl.BlockSpec((dm, tn), lambda b, i, j: (0, j)),
        ],
        out_specs=out_specs,
        out_shape=out_shape,
        scratch_shapes=[pltpu.VMEM((tm, dm), BF16), pltpu.VMEM((tn // LANES, tm, LANES), F32),
                        pltpu.VMEM((tn // LANES, tm, LANES), F32)],
        compiler_params=_cparams(("parallel", "parallel", "arbitrary")),
        name="qkv_proj",
    )(x3, nw.reshape(1, dm), w)
    return outs[:n_grp], outs[n_grp:2 * n_grp], outs[2 * n_grp:]


def _attn_kernel(q_ref, k_ref, v_ref, kp_ref, vp_ref, o_ref, st_ref, *, n_qblocks, span):
    blk, dh = ATT_BLOCK, ATT_HEAD_DIM
    n = pl.program_id(2)
    row = lax.broadcasted_iota(jnp.int32, (blk, 2 * blk), 0)
    col = lax.broadcasted_iota(jnp.int32, (blk, 2 * blk), 1)
    dist = row + blk - col
    mask = (dist >= 0) & (dist <= span)
    mask_first = mask & ((col >= blk) | (n > 0))
    lane = lax.broadcasted_iota(jnp.int32, (blk, LANES), 1)
    ones = jnp.ones((2 * blk, dh), BF16)

    for j in range(n_qblocks):
        rows = slice(j * blk, (j + 1) * blk)
        tile = jnp.zeros((blk, LANES), F32)
        for h in range(ATT_KV_HEADS):
            sl = slice(h * dh, (h + 1) * dh)
            q = q_ref[rows, sl]
            if j == 0:
                k2 = jnp.concatenate([kp_ref[:, sl], k_ref[rows, sl]], axis=0)
                v2 = jnp.concatenate([vp_ref[:, sl], v_ref[rows, sl]], axis=0)
            else:
                rows2 = slice((j - 1) * blk, (j + 1) * blk)
                k2 = k_ref[rows2, sl]
                v2 = v_ref[rows2, sl]
            s = jnp.where(mask_first if j == 0 else mask, _dot_nt(q, k2), -jnp.inf)
            m = jnp.max(s, axis=-1, keepdims=True)
            p = jnp.exp(s - m).astype(BF16)
            ov = _dot(p, jnp.concatenate([v2, ones], axis=1))
            l = ov[:, dh:]
            o_ref[rows, sl] = (ov[:, :dh] / l).astype(o_ref.dtype)
            tile = jnp.where(lane == h, m + jnp.log(l), tile)
        st_ref[rows, :] = tile


def _attn_group(q, k, v, span):
    bsz, d, m, w = q.shape
    rows = min(ATT_ROWS, m)
    qb_per_step = rows // ATT_BLOCK
    cur = pl.BlockSpec((None, None, rows, w), lambda b, r, n: (b, r, n, 0))
    prev = pl.BlockSpec((None, None, ATT_BLOCK, w), lambda b, r, n: (b, r, jnp.maximum(n * qb_per_step - 1, 0), 0))
    return pl.pallas_call(
        functools.partial(_attn_kernel, n_qblocks=qb_per_step, span=span),
        grid=(bsz, d, m // rows),
        in_specs=[cur, cur, cur, prev, prev],
        out_specs=[cur, pl.BlockSpec((None, None, rows, LANES), lambda b, r, n: (b, r, n, 0))],
        out_shape=[jax.ShapeDtypeStruct((bsz, d, m, w), BF16), jax.ShapeDtypeStruct((bsz, d, m, LANES), F32)],
        compiler_params=_cparams(("parallel", "parallel", "arbitrary")),
        name=f"attn_d{d}",
    )(q, k, v, k, v)


def _attn_out_kernel(*refs, tm, dilations):
    n_grp = len(dilations)
    o_refs, s_refs = refs[:n_grp], refs[n_grp:2 * n_grp]
    w_ref, npost_ref, x_ref, out_ref, on_ref, sn_ref, comb_ref = refs[2 * n_grp:]
    dh = ATT_HEAD_DIM

    for g, d in enumerate(dilations):
        rows = tm // d
        for r in range(d):
            if d == 1:
                sn_ref[g] = s_refs[g][r]
            else:
                sn_ref[g, pl.ds(r, rows, stride=d), :] = s_refs[g][r]
                for h in range(ATT_KV_HEADS):
                    on_ref[g, h, pl.ds(r, rows, stride=d), :] = o_refs[g][r, :, h * dh:(h + 1) * dh].astype(F32)

    lses = [sn_ref[g] for g in range(n_grp)]
    mx = functools.reduce(jnp.maximum, lses)
    es = [jnp.exp(v - mx) for v in lses]
    inv = 1.0 / functools.reduce(lambda a, b: a + b, es)
    wts = [e * inv for e in es]
    y = None
    for h in range(ATT_KV_HEADS):
        sl = slice(h * dh, (h + 1) * dh)
        acc = None
        for g, d in enumerate(dilations):
            og = o_refs[g][0, :, sl].astype(F32) if d == 1 else on_ref[g, h]
            term = wts[g][:, h:h + 1] * og
            acc = term if acc is None else acc + term
        comb_ref[:, sl] = acc.astype(BF16)
        if h % 2 == 1:
            pair = slice((h - 1) * dh, (h + 1) * dh)
            part = _dot(comb_ref[:, pair], w_ref[pair, :])
            y = part if y is None else y + part
    out_ref[...] = x_ref[...] + _rms(y, npost_ref[...])


def _attn_out(os, sts, w, npost, x3):
    bsz, s, dm = x3.shape
    tm = min(ATTN_OUT_TILE, s)
    kw = ATT_KV_WIDTH
    dilations = tuple(o.shape[1] for o in os)
    n_grp = len(dilations)
    grp_blk = lambda d, width: pl.BlockSpec((None, d, tm // d, width), lambda b, i: (b, 0, i, 0))
    row_blk = pl.BlockSpec((None, tm, dm), lambda b, i: (b, i, 0))
    return pl.pallas_call(
        functools.partial(_attn_out_kernel, tm=tm, dilations=dilations),
        grid=(bsz, s // tm),
        in_specs=[grp_blk(d, kw) for d in dilations] + [grp_blk(d, LANES) for d in dilations] + [
            pl.BlockSpec((kw, dm), lambda b, i: (0, 0)),
            pl.BlockSpec((1, dm), lambda b, i: (0, 0)),
            row_blk,
        ],
        out_specs=row_blk,
        out_shape=jax.ShapeDtypeStruct((bsz, s, dm), F32),
        scratch_shapes=[pltpu.VMEM((n_grp, ATT_KV_HEADS, tm, LANES), F32),
                        pltpu.VMEM((n_grp, tm, LANES), F32),
                        pltpu.VMEM((tm, kw), BF16)],
        compiler_params=_cparams(("parallel", "parallel")),
        name="attn_out",
    )(*os, *sts, w, npost.reshape(1, dm), x3)


def _mixer_ab(h2, bsz, s, nw_pre, nw_post, w_in, w_out, hgrn_lb, layer_idx, hg_norm_w,
              conv_w, conv_b, dt_bias, a_log, d_skip, ssm_norm_w):
    dt_col = 4 * HG_WIDTH + SSM_INNER + SSM_CONV_CH
    w_dt = jnp.pad(w_in[:, dt_col:], ((0, 0), (0, LANES - SSM_HEADS))).astype(BF16)
    main, aux = _ab_proj(h2, nw_pre, w_in.astype(BF16), w_dt)
    main3 = main.reshape(bsz, s, AB_MAIN_COLS)
    aux3 = aux.reshape(bsz, s, AB_AUX_COLS)
    o_a = _hgrn2(main3, aux3, hgrn_lb, hg_norm_w, layer_idx)
    o_b = _ssd(main3, aux3, conv_w, conv_b, dt_bias, a_log, d_skip, ssm_norm_w)
    t = bsz * s
    return _outproj_res([o_a.reshape(t, HG_WIDTH), o_b.reshape(t, SSM_INNER)], w_out.astype(BF16), nw_post, h2)


def _mixer_c(h2, bsz, s, nw_pre, nw_post, w_in, w_out):
    dm = h2.shape[1]
    x3 = h2.reshape(bsz, s, dm)
    qs, ks, vs = _qkv_proj(x3, nw_pre, w_in.astype(BF16))
    os, sts = [], []
    for g, (window, dilation) in enumerate(ATT_PATTERNS):
        o, st = _attn_group(qs[g], ks[g], vs[g], window // dilation)
        os.append(o)
        sts.append(st)
    return _attn_out(os, sts, w_out.astype(BF16), nw_post, x3).reshape(bsz * s, dm)


def kernel(x, norm_pre, norm_post, ffn_w_gate, ffn_w_up, ffn_w_down, ab_w_in, ab_w_out, hgrn_lb, hgrn_norm_w,
           ssm_conv_w, ssm_conv_b, ssm_dt_bias, ssm_A_log, ssm_D, ssm_norm_w, att_w_in, att_w_out):
    bsz, s, d = x.shape
    depth = norm_pre.shape[0]
    h2 = x.reshape(bsz * s, d)

    wg, wu, wd = ffn_w_gate.astype(BF16), ffn_w_up.astype(BF16), ffn_w_down.astype(BF16)

    def ffn(h2, layer, k):
        return _ffn(h2, norm_pre[layer, 2 * k], norm_post[layer, 2 * k], wg, wu, wd, layer, k)

    for layer in range(depth):
        h2 = ffn(h2, layer, 0)
        if layer % 2 == 0:
            e = layer // 2
            h2 = _mixer_ab(h2, bsz, s, norm_pre[layer, 1], norm_post[layer, 1], ab_w_in[e], ab_w_out[e],
                           hgrn_lb, e, hgrn_norm_w[e], ssm_conv_w[e], ssm_conv_b[e], ssm_dt_bias[e],
                           ssm_A_log[e], ssm_D[e], ssm_norm_w[e])
        else:
            o_idx = layer // 2
            h2 = _mixer_c(h2, bsz, s, norm_pre[layer, 1], norm_post[layer, 1], att_w_in[o_idx], att_w_out[o_idx])
        h2 = ffn(h2, layer, 1)
    return h2.reshape(bsz, s, d)
```

```python
import functools

import jax
import jax.numpy as jnp
from jax import lax
from jax.experimental import pallas as pl
from jax.experimental.pallas import tpu as pltpu

F32 = jnp.float32
BF16 = jnp.bfloat16
EPS = 1e-6
FFN_RES = 0.5

HG_HEADS = 8
HG_DK = 128
HG_WIDTH = HG_HEADS * HG_DK
HG_CHUNK = 64
HG_INTRA_ROWS = 128

SSM_INNER = 1024
SSM_HEAD_DIM = 64
SSM_HEADS = SSM_INNER // SSM_HEAD_DIM
SSM_GROUPS = 4
SSM_HPG = SSM_HEADS // SSM_GROUPS
SSM_STATE = 128
SSM_CONV = 4
SSM_CHUNK = 128
SSM_CONV_CH = SSM_INNER + 2 * SSM_GROUPS * SSM_STATE

ATT_PATTERNS = ((128, 1), (512, 4), (2048, 16))
ATT_GROUPS = 3
ATT_KV_HEADS = 16
ATT_HEAD_DIM = 128
ATT_BLOCK = 128
ATT_Q_WIDTH = ATT_GROUPS * ATT_KV_HEADS * ATT_HEAD_DIM
ATT_KV_WIDTH = ATT_KV_HEADS * ATT_HEAD_DIM
ATT_IN = ATT_Q_WIDTH + 2 * ATT_KV_WIDTH

LANES = 128
SUBLANES = 8
VMEM_LIMIT_BYTES = 56 * 1024 * 1024
FFN_VMEM_LIMIT_BYTES = 60 * 1024 * 1024

TOKEN_TILE = 512
FFN_TOKEN_TILE = 1024
PROJ_TOKEN_TILE = 1024
QKV_COL_TILE = 512
FF_TILE = 512
PROJ_COL_TILE = 1024
MIXER_ROWS = 512
ATT_ROWS = 512
ATTN_OUT_TILE = 256

AB_MAIN_COLS = 4 * HG_WIDTH + SSM_CONV_CH
AB_AUX_COLS = HG_WIDTH + LANES


def _cparams(semantics, vmem_limit_bytes=VMEM_LIMIT_BYTES):
    return pltpu.CompilerParams(dimension_semantics=semantics, vmem_limit_bytes=vmem_limit_bytes)


def _rms(x, w):
    ms = jnp.mean(x * x, axis=-1, keepdims=True)
    return x * lax.rsqrt(ms + EPS) * w


def _silu(x):
    return x * jax.nn.sigmoid(x)


def _dot(a, b):
    return jnp.dot(a, b, preferred_element_type=F32)


def _dot_nt(a, b):
    return lax.dot_general(a, b, (((1,), (1,)), ((), ())), preferred_element_type=F32)


def _dot_tn(a, b):
    return lax.dot_general(a, b, (((0,), (0,)), ((), ())), preferred_element_type=F32)


def _split3(x):
    hi = x.astype(BF16)
    r = x - hi.astype(F32)
    mid = r.astype(BF16)
    lo = (r - mid.astype(F32)).astype(BF16)
    return hi, mid, lo


def _sel_dot(sel, x):
    hi, mid, lo = _split3(x)
    return _dot(sel, hi) + _dot(sel, mid) + _dot(sel, lo)


def _dot_sel(x, sel):
    hi, mid, lo = _split3(x)
    return _dot(hi, sel) + _dot(mid, sel) + _dot(lo, sel)


def _dot_tn_sel(x, sel):
    hi, mid, lo = _split3(x)
    return _dot_tn(hi, sel) + _dot_tn(mid, sel) + _dot_tn(lo, sel)


def _ffn_kernel(x_ref, npre_ref, npost_ref, wg_ref, wu_ref, wd_ref, o_ref, h_ref):
    j = pl.program_id(1)
    last = pl.num_programs(1) - 1

    def chunk_product(h):
        g = _dot(h, wg_ref[...])
        u = _dot(h, wu_ref[...])
        return _dot((_silu(g) * u).astype(BF16), wd_ref[...])

    @pl.when(j == 0)
    def _():
        h = _rms(x_ref[...], npre_ref[...]).astype(BF16)
        h_ref[...] = h
        o_ref[...] = chunk_product(h)

    @pl.when((j > 0) & (j < last))
    def _():
        o_ref[...] += chunk_product(h_ref[...])

    @pl.when(j == last)
    def _():
        y = o_ref[...] + chunk_product(h_ref[...])
        o_ref[...] = x_ref[...] + _rms(y, FFN_RES * npost_ref[...])


def _ffn(x2, npre, npost, wg, wu, wd, layer, k):
    t, d = x2.shape
    f = wg.shape[-1]
    tm, tf = min(FFN_TOKEN_TILE, t), FF_TILE
    assert f // tf >= 2
    return pl.pallas_call(
        _ffn_kernel,
        grid=(t // tm, f // tf),
        in_specs=[
            pl.BlockSpec((tm, d), lambda i, j: (i, 0)),
            pl.BlockSpec((1, d), lambda i, j: (0, 0)),
            pl.BlockSpec((1, d), lambda i, j: (0, 0)),
            pl.BlockSpec((None, None, d, tf), lambda i, j: (layer, k, 0, j)),
            pl.BlockSpec((None, None, d, tf), lambda i, j: (layer, k, 0, j)),
            pl.BlockSpec((None, None, tf, d), lambda i, j: (layer, k, j, 0)),
        ],
        out_specs=pl.BlockSpec((tm, d), lambda i, j: (i, 0)),
        out_shape=jax.ShapeDtypeStruct((t, d), F32),
        scratch_shapes=[pltpu.VMEM((tm, d), BF16)],
        compiler_params=_cparams(("parallel", "arbitrary"), FFN_VMEM_LIMIT_BYTES),
        name="ffn",
    )(x2, npre.reshape(1, d), npost.reshape(1, d), wg, wu, wd)


def _ab_proj_kernel(x_ref, nw_ref, w_ref, whf_ref, wdt_ref, o_ref, oaux_ref, h_ref):
    j = pl.program_id(1)

    @pl.when(j == 0)
    def _():
        h = _rms(x_ref[...], nw_ref[...]).astype(BF16)
        h_ref[...] = h
        oaux_ref[:, :HG_WIDTH] = _dot(h, whf_ref[...])
        oaux_ref[:, HG_WIDTH:] = _dot(h, wdt_ref[...])
        o_ref[...] = _dot(h, w_ref[...]).astype(o_ref.dtype)

    @pl.when(j > 0)
    def _():
        o_ref[...] = _dot(h_ref[...], w_ref[...]).astype(o_ref.dtype)


def _ab_proj(x2, nw, w_in, w_dt):
    t, d = x2.shape
    tm, tn = min(PROJ_TOKEN_TILE, t), PROJ_COL_TILE
    assert tn == HG_WIDTH
    n_main = AB_MAIN_COLS // tn
    return pl.pallas_call(
        _ab_proj_kernel,
        grid=(t // tm, n_main),
        in_specs=[
            pl.BlockSpec((tm, d), lambda i, j: (i, 0)),
            pl.BlockSpec((1, d), lambda i, j: (0, 0)),
            pl.BlockSpec((d, tn), lambda i, j: (0, jnp.where(j >= 1, j + 1, j))),
            pl.BlockSpec((d, tn), lambda i, j: (0, 1)),
            pl.BlockSpec((d, LANES), lambda i, j: (0, 0)),
        ],
        out_specs=[pl.BlockSpec((tm, tn), lambda i, j: (i, j)),
                   pl.BlockSpec((tm, AB_AUX_COLS), lambda i, j: (i, 0))],
        out_shape=[jax.ShapeDtypeStruct((t, AB_MAIN_COLS), BF16), jax.ShapeDtypeStruct((t, AB_AUX_COLS), F32)],
        scratch_shapes=[pltpu.VMEM((tm, d), BF16)],
        compiler_params=_cparams(("parallel", "arbitrary")),
        name="ab_proj",
    )(x2, nw.reshape(1, d), w_in, w_in, w_dt)


def _outproj_kernel(*refs, n_parts):
    parts = refs[:n_parts]
    ws = refs[n_parts:2 * n_parts]
    npost_ref, x_ref, o_ref = refs[2 * n_parts:]
    y = _dot(parts[0][...], ws[0][...])
    for p_ref, w_ref in zip(parts[1:], ws[1:]):
        y = y + _dot(p_ref[...], w_ref[...])
    o_ref[...] = x_ref[...] + _rms(y, npost_ref[...])


def _outproj_res(parts, w, npost, x2):
    t, d = x2.shape
    tm = min(TOKEN_TILE, t)
    n_parts = len(parts)
    kp = w.shape[0] // n_parts
    in_specs = [pl.BlockSpec((tm, kp), lambda i: (i, 0)) for _ in parts]
    in_specs += [pl.BlockSpec((kp, d), functools.partial(lambda i, p: (p, 0), p=p)) for p in range(n_parts)]
    in_specs += [pl.BlockSpec((1, d), lambda i: (0, 0)), pl.BlockSpec((tm, d), lambda i: (i, 0))]
    return pl.pallas_call(
        functools.partial(_outproj_kernel, n_parts=n_parts),
        grid=(t // tm,),
        in_specs=in_specs,
        out_specs=pl.BlockSpec((tm, d), lambda i: (i, 0)),
        out_shape=jax.ShapeDtypeStruct((t, d), F32),
        compiler_params=_cparams(("parallel",)),
        name="outproj_res",
    )(*parts, *([w] * n_parts), npost.reshape(1, d), x2)


def _hgrn_kernel(q_ref, f_ref, i_ref, g_ref, lb_ref, nw_ref, o_ref, st_ref, *, layer_idx, n_chunks):
    c_len = HG_CHUNK

    @pl.when(pl.program_id(1) == 0)
    def _():
        st_ref[...] = jnp.zeros_like(st_ref)

    lbs = lb_ref[...]
    ex = jnp.exp(lbs - jnp.max(lbs, axis=0, keepdims=True))
    sm = ex / jnp.sum(ex, axis=0, keepdims=True)
    lb = jnp.sum(sm[:layer_idx + 1], axis=0, keepdims=True)

    span = HG_INTRA_ROWS
    rows_total = n_chunks * c_len
    per_span = span // c_len
    row = lax.broadcasted_iota(jnp.int32, (span, span), 0)
    col = lax.broadcasted_iota(jnp.int32, (span, span), 1)
    same_chunk = (row // c_len) == (col // c_len)
    causal = (col <= row) & same_chunk
    tril = jnp.where(causal, 1.0, 0.0).astype(BF16)

    q_intra, k_intra, q_inter, k_state, decays = [], [], [], [], []
    for s0 in range(0, rows_total, span):
        rows = slice(s0, s0 + span)
        f = lb + (1.0 - lb) * jax.nn.sigmoid(f_ref[0, rows, :])
        log_f = jnp.log(f)
        k = 1.0 - f
        cum = _sel_dot(tril, log_f)
        cum3 = cum.reshape(per_span, c_len, HG_WIDTH)
        b_mid = jnp.broadcast_to(cum3[:, c_len // 2:c_len // 2 + 1, :], cum3.shape).reshape(span, HG_WIDTH)
        last = cum3[:, c_len - 1:c_len, :]
        b_last = jnp.broadcast_to(last, cum3.shape).reshape(span, HG_WIDTH)
        q = q_ref[0, rows, :].astype(F32)
        q_intra.append((q * jnp.exp(cum - b_mid)).astype(BF16))
        k_intra.append((k * jnp.exp(b_mid - cum)).astype(BF16))
        q_inter.append((q * jnp.exp(cum)).astype(BF16))
        k_state.append((k * jnp.exp(b_last - cum)).astype(BF16))
        decays.extend(jnp.exp(last[c]) for c in range(per_span))

    for h in range(HG_HEADS):
        sl = slice(h * HG_DK, (h + 1) * HG_DK)
        state_t = st_ref[h]
        outs = []
        for si, s0 in enumerate(range(0, rows_total, span)):
            v = i_ref[0, s0:s0 + span, sl]
            scores = jnp.where(causal, _dot_nt(q_intra[si][:, sl], k_intra[si][:, sl]), 0.0)
            o_intra = _dot(scores.astype(BF16), v)
            inter = []
            for c in range(per_span):
                cr = slice(c * c_len, (c + 1) * c_len)
                inter.append(_dot_nt(q_inter[si][cr, sl], state_t.astype(BF16)))
                state_t = state_t * decays[si * per_span + c][:, sl] + _dot_tn(v[cr, :], k_state[si][cr, sl])
            outs.append(o_intra + jnp.concatenate(inter, axis=0))
        st_ref[h] = state_t
        o = jnp.concatenate(outs, axis=0)
        ms = jnp.mean(o * o, axis=-1, keepdims=True)
        gate = _silu(g_ref[0, :, sl].astype(F32))
        o_ref[0, :, sl] = (o * lax.rsqrt(ms + EPS) * nw_ref[:, sl] * gate).astype(o_ref.dtype)


def _hgrn2(main3, aux3, hgrn_lb, norm_w, layer_idx):
    bsz, s, _ = main3.shape
    rows = min(MIXER_ROWS, s)
    w = HG_WIDTH
    blk = lambda cb: pl.BlockSpec((1, rows, w), functools.partial(lambda b, n, cb: (b, n, cb), cb=cb))
    n_lb = hgrn_lb.shape[0]
    return pl.pallas_call(
        functools.partial(_hgrn_kernel, layer_idx=layer_idx, n_chunks=rows // HG_CHUNK),
        grid=(bsz, s // rows),
        in_specs=[
            blk(0),
            blk(0),
            blk(1),
            blk(2),
            pl.BlockSpec((n_lb, w), lambda b, n: (0, 0)),
            pl.BlockSpec((1, w), lambda b, n: (0, 0)),
        ],
        out_specs=pl.BlockSpec((1, rows, w), lambda b, n: (b, n, 0)),
        out_shape=jax.ShapeDtypeStruct((bsz, s, w), BF16),
        scratch_shapes=[pltpu.VMEM((HG_HEADS, HG_DK, HG_DK), F32)],
        compiler_params=_cparams(("parallel", "arbitrary")),
        name="hgrn2",
    )(main3, aux3, main3, main3, hgrn_lb, norm_w.reshape(1, w))


def _softplus(x):
    return jnp.maximum(x, 0.0) + jnp.log1p(jnp.exp(-jnp.abs(x)))


def _ssd_kernel(z_ref, xbc_ref, dt_ref, cw_ref, cb_ref, dtb_ref, alog_ref, d_ref, nw_ref, e_ref,
                o_ref, xp_ref, ht_ref, *, n_chunks):
    c_len, n_st, p_dim = SSM_CHUNK, SSM_STATE, SSM_HEAD_DIM
    halo = SUBLANES
    gw = SSM_HPG * p_dim
    rows_total = n_chunks * c_len

    @pl.when(pl.program_id(1) == 0)
    def _():
        xp_ref[0:halo, :] = jnp.zeros((halo, SSM_CONV_CH), F32)
        ht_ref[...] = jnp.zeros_like(ht_ref)

    @pl.when(pl.program_id(1) > 0)
    def _():
        xp_ref[0:halo, :] = xp_ref[rows_total:rows_total + halo, :]

    xp_ref[halo:, :] = xbc_ref[0].astype(F32)

    row = lax.broadcasted_iota(jnp.int32, (c_len, c_len), 0)
    col = lax.broadcasted_iota(jnp.int32, (c_len, c_len), 1)
    causal = col <= row
    tril = jnp.where(causal, 1.0, 0.0).astype(BF16)
    triu = jnp.where(row <= col, 1.0, 0.0).astype(BF16)
    lane = lax.broadcasted_iota(jnp.int32, (c_len, LANES), 1)
    first_half = lane < p_dim
    expand = e_ref[...]
    neg_a = -jnp.exp(alog_ref[...])
    cw = cw_ref[...]
    cb = cb_ref[...]

    def chunk(c, carry):
        r0 = pl.multiple_of(c * c_len, c_len)
        rows = pl.ds(r0, c_len)
        full = xp_ref[pl.ds(r0, c_len + halo), :]
        conv = cb
        for j in range(SSM_CONV):
            off = halo - (SSM_CONV - 1) + j
            conv = conv + cw[j:j + 1, :] * full[off:off + c_len, :]
        act = _silu(conv)
        xs = act[:, :SSM_INNER]
        b_all = act[:, SSM_INNER:SSM_INNER + SSM_GROUPS * n_st].astype(BF16)
        c_all = act[:, SSM_INNER + SSM_GROUPS * n_st:].astype(BF16)

        dt_h = _softplus(dt_ref[0, rows, :] + dtb_ref[...])
        dta_h = dt_h * neg_a
        acs_h = _sel_dot(tril, dta_h)
        acs_t = _dot_tn_sel(dta_h, triu)
        dt_x = _dot_sel(dt_h, expand)
        acs_x = _dot_sel(acs_h, expand)
        a_last = acs_x[c_len - 1:c_len, :]
        xdt = xs * dt_x
        dec_x = (jnp.exp(a_last - acs_x) * xdt).astype(BF16)
        e_acs = jnp.exp(acs_x)
        chunk_decay = jnp.exp(a_last)

        ys = []
        for g in range(SSM_GROUPS):
            bg = b_all[:, g * n_st:(g + 1) * n_st]
            cg = c_all[:, g * n_st:(g + 1) * n_st]
            cbm = _dot_nt(cg, bg)
            y_pairs = []
            for pair in range(SSM_HPG // 2):
                lo = g * gw + pair * LANES
                x_pair = xdt[:, lo:lo + LANES]
                halves = (jnp.where(first_half, x_pair, 0.0).astype(BF16),
                          jnp.where(first_half, 0.0, x_pair).astype(BF16))
                y_pair = None
                for sub in range(2):
                    hd = g * SSM_HPG + pair * 2 + sub
                    seg = acs_h[:, hd:hd + 1] - acs_t[hd:hd + 1, :]
                    decay_ls = jnp.exp(jnp.where(causal, seg, -jnp.inf))
                    term = _dot((cbm * decay_ls).astype(BF16), halves[sub])
                    y_pair = term if y_pair is None else y_pair + term
                y_pairs.append(y_pair)
            y_diag = jnp.concatenate(y_pairs, axis=-1)
            gs = slice(g * gw, (g + 1) * gw)
            h_t = ht_ref[g]
            y_off = _dot(cg, h_t.astype(BF16)) * e_acs[:, gs]
            ht_ref[g] = h_t * chunk_decay[:, gs] + _dot_tn(bg, dec_x[:, gs])
            ys.append(y_diag + y_off)
        y = jnp.concatenate(ys, axis=-1) + xs * d_ref[...]
        y = y * _silu(z_ref[0, rows, :].astype(F32))
        normed = []
        for g in range(SSM_GROUPS):
            yg = y[:, g * gw:(g + 1) * gw]
            ms = jnp.mean(yg * yg, axis=-1, keepdims=True)
            normed.append(yg * lax.rsqrt(ms + EPS))
        o_ref[0, rows, :] = (jnp.concatenate(normed, axis=-1) * nw_ref[...]).astype(o_ref.dtype)
        return carry

    lax.fori_loop(0, n_chunks, chunk, 0)


def _ssd(main3, aux3, conv_w, conv_b, dt_bias, a_log, d_skip, norm_w):
    bsz, s, _ = main3.shape
    rows = min(MIXER_ROWS, s)
    inner, ch = SSM_INNER, SSM_CONV_CH
    pad = LANES - SSM_HEADS
    dtb = jnp.pad(dt_bias, (0, pad)).reshape(1, LANES)
    alog = jnp.pad(a_log, (0, pad)).reshape(1, LANES)
    d_x = jnp.repeat(d_skip, SSM_HEAD_DIM).reshape(1, inner)
    head_of_lane = jnp.arange(inner) // SSM_HEAD_DIM
    expand = (jnp.arange(LANES)[:, None] == head_of_lane[None, :]).astype(BF16)
    const = lambda shape: pl.BlockSpec(shape, lambda b, n: (0, 0))
    return pl.pallas_call(
        functools.partial(_ssd_kernel, n_chunks=rows // SSM_CHUNK),
        grid=(bsz, s // rows),
        in_specs=[
            pl.BlockSpec((1, rows, inner), lambda b, n: (b, n, 3)),
            pl.BlockSpec((1, rows, ch), lambda b, n: (b, n, 2)),
            pl.BlockSpec((1, rows, LANES), lambda b, n: (b, n, HG_WIDTH // LANES)),
            const((SSM_CONV, ch)), const((1, ch)), const((1, LANES)), const((1, LANES)),
            const((1, inner)), const((1, inner)), const((LANES, inner)),
        ],
        out_specs=pl.BlockSpec((1, rows, inner), lambda b, n: (b, n, 0)),
        out_shape=jax.ShapeDtypeStruct((bsz, s, inner), BF16),
        scratch_shapes=[pltpu.VMEM((rows + SUBLANES, ch), F32),
                        pltpu.VMEM((SSM_GROUPS, SSM_STATE, SSM_HPG * SSM_HEAD_DIM), F32)],
        compiler_params=_cparams(("parallel", "arbitrary")),
        name="ssd",
    )(main3, main3, aux3, conv_w, conv_b.reshape(1, ch), dtb, alog, d_x, norm_w.reshape(1, inner), expand)


def _qkv_kernel(x_ref, nw_ref, w_ref, *refs, tm, tn, dilations):
    n_grp = len(dilations)
    q_refs, k_refs, v_refs = refs[:n_grp], refs[n_grp:2 * n_grp], refs[2 * n_grp:3 * n_grp]
    h_ref, acc_ref, mid_ref = refs[3 * n_grp:]
    j = pl.program_id(2)
    per = ATT_KV_WIDTH // tn
    n_slabs = tn // LANES
    d_mid, d_far = dilations[1], dilations[2]
    fan = d_far // d_mid
    rows_mid, rows_far = tm // d_mid, tm // d_far

    def project_and_emit(targets, scale, first=False):
        if first:
            h = _rms(x_ref[...], nw_ref[...]).astype(BF16)
            h_ref[...] = h
        else:
            h = h_ref[...]
        acc = _dot(h, w_ref[...])
        if scale != 1.0:
            acc = acc * scale
        by_d = {d: o_ref for o_ref, d in targets}
        if 1 in by_d:
            by_d[1][0] = acc.astype(BF16)
        if d_mid not in by_d and d_far not in by_d:
            return
        for c in range(n_slabs):
            acc_ref[c] = acc[:, c * LANES:(c + 1) * LANES]
        for r in range(d_mid):
            for c in range(n_slabs):
                val = acc_ref[c, pl.ds(r, rows_mid, stride=d_mid), :]
                if d_mid in by_d:
                    by_d[d_mid][r, :, c * LANES:(c + 1) * LANES] = val.astype(BF16)
                if d_far in by_d:
                    mid_ref[c, r * rows_mid:(r + 1) * rows_mid, :] = val
        if d_far in by_d:
            for r in range(d_mid):
                for k in range(fan):
                    for c in range(n_slabs):
                        val = mid_ref[c, pl.ds(r * rows_mid + k, rows_far, stride=fan), :]
                        by_d[d_far][r + d_mid * k, :, c * LANES:(c + 1) * LANES] = val.astype(BF16)

    @pl.when(j == 0)
    def _():
        project_and_emit([(q_refs[0], dilations[0])], ATT_HEAD_DIM ** -0.5, first=True)

    for g, d in enumerate(dilations):
        @pl.when((j >= max(g * per, 1)) & (j < (g + 1) * per))
        def _(g=g, d=d):
            project_and_emit([(q_refs[g], d)], ATT_HEAD_DIM ** -0.5)

    @pl.when((j >= n_grp * per) & (j < (n_grp + 1) * per))
    def _():
        project_and_emit(list(zip(k_refs, dilations)), 1.0)

    @pl.when(j >= (n_grp + 1) * per)
    def _():
        project_and_emit(list(zip(v_refs, dilations)), 1.0)


def _qkv_proj(x3, nw, w):
    bsz, s, dm = x3.shape
    tm, tn = min(PROJ_TOKEN_TILE, s), QKV_COL_TILE
    per = ATT_KV_WIDTH // tn
    dilations = tuple(d for _, d in ATT_PATTERNS)
    n_grp = len(dilations)
    assert dilations[0] == 1 and dilations[2] % dilations[1] == 0

    def out_spec(d, j0):
        return pl.BlockSpec((None, d, tm // d, tn),
                            lambda b, i, j: (b, 0, i, jnp.clip(j - j0, 0, per - 1)))

    starts = [g * per for g in range(n_grp)] + [n_grp * per] * n_grp + [(n_grp + 1) * per] * n_grp
    out_specs = [out_spec(d, j0) for d, j0 in zip(dilations * 3, starts)]
    out_shape = [jax.ShapeDtypeStruct((bsz, d, s // d, ATT_KV_WIDTH), BF16) for d in dilations * 3]
    outs = pl.pallas_call(
        functools.partial(_qkv_kernel, tm=tm, tn=tn, dilations=dilations),
        grid=(bsz, s // tm, w.shape[1] // tn),
        in_specs=[
            pl.BlockSpec((None, tm, dm), lambda b, i, j: (b, i, 0)),
            pl.BlockSpec((1, dm), lambda b, i, j: (0, 0)),
            pl.BlockSpec((dm, tn), lambda b, i, j: (0, j)),
        ],
        out_specs=out_specs,
        out_shape=out_shape,
        scratch_shapes=[pltpu.VMEM((tm, dm), BF16), pltpu.VMEM((tn // LANES, tm, LANES), F32),
                        pltpu.VMEM((tn // LANES, tm, LANES), F32)],
        compiler_params=_cparams(("parallel", "parallel", "arbitrary")),
        name="qkv_proj",
    )(x3, nw.reshape(1, dm), w)
    return outs[:n_grp], outs[n_grp:2 * n_grp], outs[2 * n_grp:]


def _attn_kernel(q_ref, k_ref, v_ref, kp_ref, vp_ref, o_ref, st_ref, *, n_qblocks, span):
    blk, dh = ATT_BLOCK, ATT_HEAD_DIM
    n = pl.program_id(2)
    row = lax.broadcasted_iota(jnp.int32, (blk, 2 * blk), 0)
    col = lax.broadcasted_iota(jnp.int32, (blk, 2 * blk), 1)
    dist = row + blk - col
    mask = (dist >= 0) & (dist <= span)
    mask_first = mask & ((col >= blk) | (n > 0))
    lane = lax.broadcasted_iota(jnp.int32, (blk, LANES), 1)
    ones = jnp.ones((2 * blk, dh), BF16)

    for j in range(n_qblocks):
        rows = slice(j * blk, (j + 1) * blk)
        tile = jnp.zeros((blk, LANES), F32)
        for h in range(ATT_KV_HEADS):
            sl = slice(h * dh, (h + 1) * dh)
            q = q_ref[rows, sl]
            if j == 0:
                k2 = jnp.concatenate([kp_ref[:, sl], k_ref[rows, sl]], axis=0)
                v2 = jnp.concatenate([vp_ref[:, sl], v_ref[rows, sl]], axis=0)
            else:
                rows2 = slice((j - 1) * blk, (j + 1) * blk)
                k2 = k_ref[rows2, sl]
                v2 = v_ref[rows2, sl]
            s = jnp.where(mask_first if j == 0 else mask, _dot_nt(q, k2), -jnp.inf)
            m = jnp.max(s, axis=-1, keepdims=True)
            p = jnp.exp(s - m).astype(BF16)
            ov = _dot(p, jnp.concatenate([v2, ones], axis=1))
            l = ov[:, dh:]
            o_ref[rows, sl] = (ov[:, :dh] / l).astype(o_ref.dtype)
            tile = jnp.where(lane == h, m + jnp.log(l), tile)
        st_ref[rows, :] = tile


def _attn_group(q, k, v, span):
    bsz, d, m, w = q.shape
    rows = min(ATT_ROWS, m)
    qb_per_step = rows // ATT_BLOCK
    cur = pl.BlockSpec((None, None, rows, w), lambda b, r, n: (b, r, n, 0))
    prev = pl.BlockSpec((None, None, ATT_BLOCK, w), lambda b, r, n: (b, r, jnp.maximum(n * qb_per_step - 1, 0), 0))
    return pl.pallas_call(
        functools.partial(_attn_kernel, n_qblocks=qb_per_step, span=span),
        grid=(bsz, d, m // rows),
        in_specs=[cur, cur, cur, prev, prev],
        out_specs=[cur, pl.BlockSpec((None, None, rows, LANES), lambda b, r, n: (b, r, n, 0))],
        out_shape=[jax.ShapeDtypeStruct((bsz, d, m, w), BF16), jax.ShapeDtypeStruct((bsz, d, m, LANES), F32)],
        compiler_params=_cparams(("parallel", "parallel", "arbitrary")),
        name=f"attn_d{d}",
    )(q, k, v, k, v)


def _attn_out_kernel(*refs, tm, dilations):
    n_grp = len(dilations)
    o_refs, s_refs = refs[:n_grp], refs[n_grp:2 * n_grp]
    w_ref, npost_ref, x_ref, out_ref, on_ref, sn_ref, comb_ref = refs[2 * n_grp:]
    dh = ATT_HEAD_DIM

    for g, d in enumerate(dilations):
        rows = tm // d
        for r in range(d):
            if d == 1:
                sn_ref[g] = s_refs[g][r]
            else:
                sn_ref[g, pl.ds(r, rows, stride=d), :] = s_refs[g][r]
                for h in range(ATT_KV_HEADS):
                    on_ref[g, h, pl.ds(r, rows, stride=d), :] = o_refs[g][r, :, h * dh:(h + 1) * dh].astype(F32)

    lses = [sn_ref[g] for g in range(n_grp)]
    mx = functools.reduce(jnp.maximum, lses)
    es = [jnp.exp(v - mx) for v in lses]
    inv = 1.0 / functools.reduce(lambda a, b: a + b, es)
    wts = [e * inv for e in es]
    y = None
    for h in range(ATT_KV_HEADS):
        sl = slice(h * dh, (h + 1) * dh)
        acc = None
        for g, d in enumerate(dilations):
            og = o_refs[g][0, :, sl].astype(F32) if d == 1 else on_ref[g, h]
            term = wts[g][:, h:h + 1] * og
            acc = term if acc is None else acc + term
        comb_ref[:, sl] = acc.astype(BF16)
        if h % 2 == 1:
            pair = slice((h - 1) * dh, (h + 1) * dh)
            part = _dot(comb_ref[:, pair], w_ref[pair, :])
            y = part if y is None else y + part
    out_ref[...] = x_ref[...] + _rms(y, npost_ref[...])


def _attn_out(os, sts, w, npost, x3):
    bsz, s, dm = x3.shape
    tm = min(ATTN_OUT_TILE, s)
    kw = ATT_KV_WIDTH
    dilations = tuple(o.shape[1] for o in os)
    n_grp = len(dilations)
    grp_blk = lambda d, width: pl.BlockSpec((None, d, tm // d, width), lambda b, i: (b, 0, i, 0))
    row_blk = pl.BlockSpec((None, tm, dm), lambda b, i: (b, i, 0))
    return pl.pallas_call(
        functools.partial(_attn_out_kernel, tm=tm, dilations=dilations),
        grid=(bsz, s // tm),
        in_specs=[grp_blk(d, kw) for d in dilations] + [grp_blk(d, LANES) for d in dilations] + [
            pl.BlockSpec((kw, dm), lambda b, i: (0, 0)),
            pl.BlockSpec((1, dm), lambda b, i: (0, 0)),
            row_blk,
        ],
        out_specs=row_blk,
        out_shape=jax.ShapeDtypeStruct((bsz, s, dm), F32),
        scratch_shapes=[pltpu.VMEM((n_grp, ATT_KV_HEADS, tm, LANES), F32),
                        pltpu.VMEM((n_grp, tm, LANES), F32),
                        pltpu.VMEM((tm, kw), BF16)],
        compiler_params=_cparams(("parallel", "parallel")),
        name="attn_out",
    )(*os, *sts, w, npost.reshape(1, dm), x3)


def _mixer_ab(h2, bsz, s, nw_pre, nw_post, w_in, w_out, hgrn_lb, layer_idx, hg_norm_w,
              conv_w, conv_b, dt_bias, a_log, d_skip, ssm_norm_w):
    dt_col = 4 * HG_WIDTH + SSM_INNER + SSM_CONV_CH
    w_dt = jnp.pad(w_in[:, dt_col:], ((0, 0), (0, LANES - SSM_HEADS))).astype(BF16)
    main, aux = _ab_proj(h2, nw_pre, w_in.astype(BF16), w_dt)
    main3 = main.reshape(bsz, s, AB_MAIN_COLS)
    aux3 = aux.reshape(bsz, s, AB_AUX_COLS)
    o_a = _hgrn2(main3, aux3, hgrn_lb, hg_norm_w, layer_idx)
    o_b = _ssd(main3, aux3, conv_w, conv_b, dt_bias, a_log, d_skip, ssm_norm_w)
    t = bsz * s
    return _outproj_res([o_a.reshape(t, HG_WIDTH), o_b.reshape(t, SSM_INNER)], w_out.astype(BF16), nw_post, h2)


def _mixer_c(h2, bsz, s, nw_pre, nw_post, w_in, w_out):
    dm = h2.shape[1]
    x3 = h2.reshape(bsz, s, dm)
    qs, ks, vs = _qkv_proj(x3, nw_pre, w_in.astype(BF16))
    os, sts = [], []
    for g, (window, dilation) in enumerate(ATT_PATTERNS):
        o, st = _attn_group(qs[g], ks[g], vs[g], window // dilation)
        os.append(o)
        sts.append(st)
    return _attn_out(os, sts, w_out.astype(BF16), nw_post, x3).reshape(bsz * s, dm)


def kernel(x, norm_pre, norm_post, ffn_w_gate, ffn_w_up, ffn_w_down, ab_w_in, ab_w_out, hgrn_lb, hgrn_norm_w,
           ssm_conv_w, ssm_conv_b, ssm_dt_bias, ssm_A_log, ssm_D, ssm_norm_w, att_w_in, att_w_out):
    bsz, s, d = x.shape
    depth = norm_pre.shape[0]
    h2 = x.reshape(bsz * s, d)

    wg, wu, wd = ffn_w_gate.astype(BF16), ffn_w_up.astype(BF16), ffn_w_down.astype(BF16)

    def ffn(h2, layer, k):
        return _ffn(h2, norm_pre[layer, 2 * k], norm_post[layer, 2 * k], wg, wu, wd, layer, k)

    for layer in range(depth):
        h2 = ffn(h2, layer, 0)
        if layer % 2 == 0:
            e = layer // 2
            h2 = _mixer_ab(h2, bsz, s, norm_pre[layer, 1], norm_post[layer, 1], ab_w_in[e], ab_w_out[e],
                           hgrn_lb, e, hgrn_norm_w[e], ssm_conv_w[e], ssm_conv_b[e], ssm_dt_bias[e],
                           ssm_A_log[e], ssm_D[e], ssm_norm_w[e])
        else:
            o_idx = layer // 2
            h2 = _mixer_c(h2, bsz, s, norm_pre[layer, 1], norm_post[layer, 1], att_w_in[o_idx], att_w_out[o_idx])
        h2 = ffn(h2, layer, 1)
    return h2.reshape(bsz, s, d)
```

```python
import functools

import jax
import jax.numpy as jnp
from jax import lax
from jax.experimental import pallas as pl
from jax.experimental.pallas import tpu as pltpu

F32 = jnp.float32
BF16 = jnp.bfloat16
EPS = 1e-6
FFN_RES = 0.5

HG_HEADS = 8
HG_DK = 128
HG_WIDTH = HG_HEADS * HG_DK
HG_CHUNK = 64
HG_INTRA_ROWS = 128

SSM_INNER = 1024
SSM_HEAD_DIM = 64
SSM_HEADS = SSM_INNER // SSM_HEAD_DIM
SSM_GROUPS = 4
SSM_HPG = SSM_HEADS // SSM_GROUPS
SSM_STATE = 128
SSM_CONV = 4
SSM_CHUNK = 128
SSM_CONV_CH = SSM_INNER + 2 * SSM_GROUPS * SSM_STATE

ATT_PATTERNS = ((128, 1), (512, 4), (2048, 16))
ATT_GROUPS = 3
ATT_KV_HEADS = 16
ATT_HEAD_DIM = 128
ATT_BLOCK = 128
ATT_Q_WIDTH = ATT_GROUPS * ATT_KV_HEADS * ATT_HEAD_DIM
ATT_KV_WIDTH = ATT_KV_HEADS * ATT_HEAD_DIM
ATT_IN = ATT_Q_WIDTH + 2 * ATT_KV_WIDTH

LANES = 128
SUBLANES = 8
VMEM_LIMIT_BYTES = 56 * 1024 * 1024
FFN_VMEM_LIMIT_BYTES = 60 * 1024 * 1024

TOKEN_TILE = 512
FFN_TOKEN_TILE = 1024
PROJ_TOKEN_TILE = 1024
QKV_COL_TILE = 512
FF_TILE = 512
PROJ_COL_TILE = 1024
MIXER_ROWS = 512
ATT_ROWS = 512
ATTN_OUT_TILE = 256

AB_MAIN_COLS = 4 * HG_WIDTH + SSM_CONV_CH
AB_AUX_COLS = HG_WIDTH + LANES


def _cparams(semantics, vmem_limit_bytes=VMEM_LIMIT_BYTES):
    return pltpu.CompilerParams(dimension_semantics=semantics, vmem_limit_bytes=vmem_limit_bytes)


def _rms(x, w):
    ms = jnp.mean(x * x, axis=-1, keepdims=True)
    return x * lax.rsqrt(ms + EPS) * w


def _silu(x):
    return x * jax.nn.sigmoid(x)


def _dot(a, b):
    return jnp.dot(a, b, preferred_element_type=F32)


def _dot_nt(a, b):
    return lax.dot_general(a, b, (((1,), (1,)), ((), ())), preferred_element_type=F32)


def _dot_tn(a, b):
    return lax.dot_general(a, b, (((0,), (0,)), ((), ())), preferred_element_type=F32)


def _split3(x):
    hi = x.astype(BF16)
    r = x - hi.astype(F32)
    mid = r.astype(BF16)
    lo = (r - mid.astype(F32)).astype(BF16)
    return hi, mid, lo


def _sel_dot(sel, x):
    hi, mid, lo = _split3(x)
    return _dot(sel, hi) + _dot(sel, mid) + _dot(sel, lo)


def _dot_sel(x, sel):
    hi, mid, lo = _split3(x)
    return _dot(hi, sel) + _dot(mid, sel) + _dot(lo, sel)


def _dot_tn_sel(x, sel):
    hi, mid, lo = _split3(x)
    return _dot_tn(hi, sel) + _dot_tn(mid, sel) + _dot_tn(lo, sel)


def _ffn_kernel(x_ref, npre_ref, npost_ref, wg_ref, wu_ref, wd_ref, o_ref, h_ref):
    j = pl.program_id(1)
    last = pl.num_programs(1) - 1

    def chunk_product(h):
        g = _dot(h, wg_ref[...])
        u = _dot(h, wu_ref[...])
        return _dot((_silu(g) * u).astype(BF16), wd_ref[...])

    @pl.when(j == 0)
    def _():
        h = _rms(x_ref[...], npre_ref[...]).astype(BF16)
        h_ref[...] = h
        o_ref[...] = chunk_product(h)

    @pl.when((j > 0) & (j < last))
    def _():
        o_ref[...] += chunk_product(h_ref[...])

    @pl.when(j == last)
    def _():
        y = o_ref[...] + chunk_product(h_ref[...])
        o_ref[...] = x_ref[...] + _rms(y, FFN_RES * npost_ref[...])


def _ffn(x2, npre, npost, wg, wu, wd, layer, k):
    t, d = x2.shape
    f = wd.shape[-2]
    tm, tf = min(FFN_TOKEN_TILE, t), FF_TILE
    assert f // tf >= 2
    return pl.pallas_call(
        _ffn_kernel,
        grid=(t // tm, f // tf),
        in_specs=[
            pl.BlockSpec((tm, d), lambda i, j: (i, 0)),
            pl.BlockSpec((1, d), lambda i, j: (0, 0)),
            pl.BlockSpec((1, d), lambda i, j: (0, 0)),
            pl.BlockSpec((None, None, None, d, tf), lambda i, j: (layer, k, j, 0, 0)),
            pl.BlockSpec((None, None, None, d, tf), lambda i, j: (layer, k, j, 0, 0)),
            pl.BlockSpec((None, None, tf, d), lambda i, j: (layer, k, j, 0)),
        ],
        out_specs=pl.BlockSpec((tm, d), lambda i, j: (i, 0)),
        out_shape=jax.ShapeDtypeStruct((t, d), F32),
        scratch_shapes=[pltpu.VMEM((tm, d), BF16)],
        compiler_params=_cparams(("parallel", "arbitrary"), FFN_VMEM_LIMIT_BYTES),
        name="ffn",
    )(x2, npre.reshape(1, d), npost.reshape(1, d), wg, wu, wd)


def _ab_proj_kernel(x_ref, nw_ref, w_ref, whf_ref, wdt_ref, o_ref, oaux_ref, h_ref):
    j = pl.program_id(1)

    @pl.when(j == 0)
    def _():
        h = _rms(x_ref[...], nw_ref[...]).astype(BF16)
        h_ref[...] = h
        oaux_ref[:, :HG_WIDTH] = _dot(h, whf_ref[...])
        oaux_ref[:, HG_WIDTH:] = _dot(h, wdt_ref[...])
        o_ref[...] = _dot(h, w_ref[...]).astype(o_ref.dtype)

    @pl.when(j > 0)
    def _():
        o_ref[...] = _dot(h_ref[...], w_ref[...]).astype(o_ref.dtype)


def _ab_proj(x2, nw, w_in, w_dt):
    t, d = x2.shape
    tm, tn = min(PROJ_TOKEN_TILE, t), PROJ_COL_TILE
    assert tn == HG_WIDTH
    n_main = AB_MAIN_COLS // tn
    return pl.pallas_call(
        _ab_proj_kernel,
        grid=(t // tm, n_main),
        in_specs=[
            pl.BlockSpec((tm, d), lambda i, j: (i, 0)),
            pl.BlockSpec((1, d), lambda i, j: (0, 0)),
            pl.BlockSpec((d, tn), lambda i, j: (0, jnp.where(j >= 1, j + 1, j))),
            pl.BlockSpec((d, tn), lambda i, j: (0, 1)),
            pl.BlockSpec((d, LANES), lambda i, j: (0, 0)),
        ],
        out_specs=[pl.BlockSpec((tm, tn), lambda i, j: (i, j)),
                   pl.BlockSpec((tm, AB_AUX_COLS), lambda i, j: (i, 0))],
        out_shape=[jax.ShapeDtypeStruct((t, AB_MAIN_COLS), BF16), jax.ShapeDtypeStruct((t, AB_AUX_COLS), F32)],
        scratch_shapes=[pltpu.VMEM((tm, d), BF16)],
        compiler_params=_cparams(("parallel", "arbitrary")),
        name="ab_proj",
    )(x2, nw.reshape(1, d), w_in, w_in, w_dt)


def _outproj_kernel(*refs, n_parts):
    parts = refs[:n_parts]
    ws = refs[n_parts:2 * n_parts]
    npost_ref, x_ref, o_ref = refs[2 * n_parts:]
    y = _dot(parts[0][...], ws[0][...])
    for p_ref, w_ref in zip(parts[1:], ws[1:]):
        y = y + _dot(p_ref[...], w_ref[...])
    o_ref[...] = x_ref[...] + _rms(y, npost_ref[...])


def _outproj_res(parts, w, npost, x2):
    t, d = x2.shape
    tm = min(TOKEN_TILE, t)
    n_parts = len(parts)
    kp = w.shape[0] // n_parts
    in_specs = [pl.BlockSpec((tm, kp), lambda i: (i, 0)) for _ in parts]
    in_specs += [pl.BlockSpec((kp, d), functools.partial(lambda i, p: (p, 0), p=p)) for p in range(n_parts)]
    in_specs += [pl.BlockSpec((1, d), lambda i: (0, 0)), pl.BlockSpec((tm, d), lambda i: (i, 0))]
    return pl.pallas_call(
        functools.partial(_outproj_kernel, n_parts=n_parts),
        grid=(t // tm,),
        in_specs=in_specs,
        out_specs=pl.BlockSpec((tm, d), lambda i: (i, 0)),
        out_shape=jax.ShapeDtypeStruct((t, d), F32),
        compiler_params=_cparams(("parallel",)),
        name="outproj_res",
    )(*parts, *([w] * n_parts), npost.reshape(1, d), x2)


def _hgrn_kernel(q_ref, f_ref, i_ref, g_ref, lb_ref, nw_ref, o_ref, st_ref, *, layer_idx, n_chunks):
    c_len = HG_CHUNK

    @pl.when(pl.program_id(1) == 0)
    def _():
        st_ref[...] = jnp.zeros_like(st_ref)

    lbs = lb_ref[...]
    ex = jnp.exp(lbs - jnp.max(lbs, axis=0, keepdims=True))
    sm = ex / jnp.sum(ex, axis=0, keepdims=True)
    lb = jnp.sum(sm[:layer_idx + 1], axis=0, keepdims=True)

    span = HG_INTRA_ROWS
    rows_total = n_chunks * c_len
    per_span = span // c_len
    row = lax.broadcasted_iota(jnp.int32, (span, span), 0)
    col = lax.broadcasted_iota(jnp.int32, (span, span), 1)
    same_chunk = (row // c_len) == (col // c_len)
    causal = (col <= row) & same_chunk
    tril = jnp.where(causal, 1.0, 0.0).astype(BF16)

    q_intra, k_intra, q_inter, k_state, decays = [], [], [], [], []
    for s0 in range(0, rows_total, span):
        rows = slice(s0, s0 + span)
        f = lb + (1.0 - lb) * jax.nn.sigmoid(f_ref[0, rows, :])
        log_f = jnp.log(f)
        k = 1.0 - f
        cum = _sel_dot(tril, log_f)
        cum3 = cum.reshape(per_span, c_len, HG_WIDTH)
        b_mid = jnp.broadcast_to(cum3[:, c_len // 2:c_len // 2 + 1, :], cum3.shape).reshape(span, HG_WIDTH)
        last = cum3[:, c_len - 1:c_len, :]
        b_last = jnp.broadcast_to(last, cum3.shape).reshape(span, HG_WIDTH)
        q = q_ref[0, rows, :].astype(F32)
        q_intra.append((q * jnp.exp(cum - b_mid)).astype(BF16))
        k_intra.append((k * jnp.exp(b_mid - cum)).astype(BF16))
        q_inter.append((q * jnp.exp(cum)).astype(BF16))
        k_state.append((k * jnp.exp(b_last - cum)).astype(BF16))
        decays.extend(jnp.exp(last[c]) for c in range(per_span))

    for h in range(HG_HEADS):
        sl = slice(h * HG_DK, (h + 1) * HG_DK)
        state_t = st_ref[h]
        outs = []
        for si, s0 in enumerate(range(0, rows_total, span)):
            v = i_ref[0, s0:s0 + span, sl]
            scores = jnp.where(causal, _dot_nt(q_intra[si][:, sl], k_intra[si][:, sl]), 0.0)
            o_intra = _dot(scores.astype(BF16), v)
            inter = []
            for c in range(per_span):
                cr = slice(c * c_len, (c + 1) * c_len)
                inter.append(_dot_nt(q_inter[si][cr, sl], state_t.astype(BF16)))
                state_t = state_t * decays[si * per_span + c][:, sl] + _dot_tn(v[cr, :], k_state[si][cr, sl])
            outs.append(o_intra + jnp.concatenate(inter, axis=0))
        st_ref[h] = state_t
        o = jnp.concatenate(outs, axis=0)
        ms = jnp.mean(o * o, axis=-1, keepdims=True)
        gate = _silu(g_ref[0, :, sl].astype(F32))
        o_ref[0, :, sl] = (o * lax.rsqrt(ms + EPS) * nw_ref[:, sl] * gate).astype(o_ref.dtype)


def _hgrn2(main3, aux3, hgrn_lb, norm_w, layer_idx):
    bsz, s, _ = main3.shape
    rows = min(MIXER_ROWS, s)
    w = HG_WIDTH
    blk = lambda cb: pl.BlockSpec((1, rows, w), functools.partial(lambda b, n, cb: (b, n, cb), cb=cb))
    n_lb = hgrn_lb.shape[0]
    return pl.pallas_call(
        functools.partial(_hgrn_kernel, layer_idx=layer_idx, n_chunks=rows // HG_CHUNK),
        grid=(bsz, s // rows),
        in_specs=[
            blk(0),
            blk(0),
            blk(1),
            blk(2),
            pl.BlockSpec((n_lb, w), lambda b, n: (0, 0)),
            pl.BlockSpec((1, w), lambda b, n: (0, 0)),
        ],
        out_specs=pl.BlockSpec((1, rows, w), lambda b, n: (b, n, 0)),
        out_shape=jax.ShapeDtypeStruct((bsz, s, w), BF16),
        scratch_shapes=[pltpu.VMEM((HG_HEADS, HG_DK, HG_DK), F32)],
        compiler_params=_cparams(("parallel", "arbitrary")),
        name="hgrn2",
    )(main3, aux3, main3, main3, hgrn_lb, norm_w.reshape(1, w))


def _softplus(x):
    return jnp.maximum(x, 0.0) + jnp.log1p(jnp.exp(-jnp.abs(x)))


def _ssd_kernel(z_ref, xbc_ref, dt_ref, cw_ref, cb_ref, dtb_ref, alog_ref, d_ref, nw_ref, e_ref,
                o_ref, xp_ref, ht_ref, *, n_chunks):
    c_len, n_st, p_dim = SSM_CHUNK, SSM_STATE, SSM_HEAD_DIM
    halo = SUBLANES
    gw = SSM_HPG * p_dim
    rows_total = n_chunks * c_len

    @pl.when(pl.program_id(1) == 0)
    def _():
        xp_ref[0:halo, :] = jnp.zeros((halo, SSM_CONV_CH), F32)
        ht_ref[...] = jnp.zeros_like(ht_ref)

    @pl.when(pl.program_id(1) > 0)
    def _():
        xp_ref[0:halo, :] = xp_ref[rows_total:rows_total + halo, :]

    xp_ref[halo:, :] = xbc_ref[0].astype(F32)

    row = lax.broadcasted_iota(jnp.int32, (c_len, c_len), 0)
    col = lax.broadcasted_iota(jnp.int32, (c_len, c_len), 1)
    causal = col <= row
    tril = jnp.where(causal, 1.0, 0.0).astype(BF16)
    triu = jnp.where(row <= col, 1.0, 0.0).astype(BF16)
    srow = lax.broadcasted_iota(jnp.int32, (c_len, c_len + halo), 0)
    scol = lax.broadcasted_iota(jnp.int32, (c_len, c_len + halo), 1)
    shifts = [jnp.where(scol == srow + halo - (SSM_CONV - 1) + j, 1.0, 0.0).astype(BF16) for j in range(SSM_CONV - 1)]
    lane = lax.broadcasted_iota(jnp.int32, (c_len, LANES), 1)
    first_half = lane < p_dim
    expand = e_ref[...]
    neg_a = -jnp.exp(alog_ref[...])
    cw = cw_ref[...]
    cb = cb_ref[...]

    def chunk(c, carry):
        r0 = pl.multiple_of(c * c_len, c_len)
        rows = pl.ds(r0, c_len)
        full = xp_ref[pl.ds(r0, c_len + halo), :]
        full_b = full.astype(BF16)
        conv = cb + cw[SSM_CONV - 1:SSM_CONV, :] * full[halo:halo + c_len, :]
        for j in range(SSM_CONV - 1):
            conv = conv + cw[j:j + 1, :] * _dot(shifts[j], full_b)
        act = _silu(conv)
        xs = act[:, :SSM_INNER]
        b_all = act[:, SSM_INNER:SSM_INNER + SSM_GROUPS * n_st].astype(BF16)
        c_all = act[:, SSM_INNER + SSM_GROUPS * n_st:].astype(BF16)

        dt_h = _softplus(dt_ref[0, rows, :] + dtb_ref[...])
        dta_h = dt_h * neg_a
        acs_h = _sel_dot(tril, dta_h)
        acs_t = _dot_tn_sel(dta_h, triu)
        dt_x = _dot_sel(dt_h, expand)
        acs_x = _dot_sel(acs_h, expand)
        a_last = acs_x[c_len - 1:c_len, :]
        xdt = xs * dt_x
        dec_x = (jnp.exp(a_last - acs_x) * xdt).astype(BF16)
        e_acs = jnp.exp(acs_x)
        chunk_decay = jnp.exp(a_last)

        ys = []
        for g in range(SSM_GROUPS):
            bg = b_all[:, g * n_st:(g + 1) * n_st]
            cg = c_all[:, g * n_st:(g + 1) * n_st]
            cbm = _dot_nt(cg, bg)
            y_pairs = []
            for pair in range(SSM_HPG // 2):
                lo = g * gw + pair * LANES
                x_pair = xdt[:, lo:lo + LANES]
                halves = (jnp.where(first_half, x_pair, 0.0).astype(BF16),
                          jnp.where(first_half, 0.0, x_pair).astype(BF16))
                y_pair = None
                for sub in range(2):
                    hd = g * SSM_HPG + pair * 2 + sub
                    seg = acs_h[:, hd:hd + 1] - acs_t[hd:hd + 1, :]
                    decay_ls = jnp.exp(jnp.where(causal, seg, -jnp.inf))
                    term = _dot((cbm * decay_ls).astype(BF16), halves[sub])
                    y_pair = term if y_pair is None else y_pair + term
                y_pairs.append(y_pair)
            y_diag = jnp.concatenate(y_pairs, axis=-1)
            gs = slice(g * gw, (g + 1) * gw)
            h_t = ht_ref[g]
            y_off = _dot(cg, h_t.astype(BF16)) * e_acs[:, gs]
            ht_ref[g] = h_t * chunk_decay[:, gs] + _dot_tn(bg, dec_x[:, gs])
            ys.append(y_diag + y_off)
        y = jnp.concatenate(ys, axis=-1) + xs * d_ref[...]
        y = y * _silu(z_ref[0, rows, :].astype(F32))
        normed = []
        for g in range(SSM_GROUPS):
            yg = y[:, g * gw:(g + 1) * gw]
            ms = jnp.mean(yg * yg, axis=-1, keepdims=True)
            normed.append(yg * lax.rsqrt(ms + EPS))
        o_ref[0, rows, :] = (jnp.concatenate(normed, axis=-1) * nw_ref[...]).astype(o_ref.dtype)
        return carry

    lax.fori_loop(0, n_chunks, chunk, 0)


def _ssd(main3, aux3, conv_w, conv_b, dt_bias, a_log, d_skip, norm_w):
    bsz, s, _ = main3.shape
    rows = min(MIXER_ROWS, s)
    inner, ch = SSM_INNER, SSM_CONV_CH
    pad = LANES - SSM_HEADS
    dtb = jnp.pad(dt_bias, (0, pad)).reshape(1, LANES)
    alog = jnp.pad(a_log, (0, pad)).reshape(1, LANES)
    d_x = jnp.repeat(d_skip, SSM_HEAD_DIM).reshape(1, inner)
    head_of_lane = jnp.arange(inner) // SSM_HEAD_DIM
    expand = (jnp.arange(LANES)[:, None] == head_of_lane[None, :]).astype(BF16)
    const = lambda shape: pl.BlockSpec(shape, lambda b, n: (0, 0))
    return pl.pallas_call(
        functools.partial(_ssd_kernel, n_chunks=rows // SSM_CHUNK),
        grid=(bsz, s // rows),
        in_specs=[
            pl.BlockSpec((1, rows, inner), lambda b, n: (b, n, 3)),
            pl.BlockSpec((1, rows, ch), lambda b, n: (b, n, 2)),
            pl.BlockSpec((1, rows, LANES), lambda b, n: (b, n, HG_WIDTH // LANES)),
            const((SSM_CONV, ch)), const((1, ch)), const((1, LANES)), const((1, LANES)),
            const((1, inner)), const((1, inner)), const((LANES, inner)),
        ],
        out_specs=pl.BlockSpec((1, rows, inner), lambda b, n: (b, n, 0)),
        out_shape=jax.ShapeDtypeStruct((bsz, s, inner), BF16),
        scratch_shapes=[pltpu.VMEM((rows + SUBLANES, ch), F32),
                        pltpu.VMEM((SSM_GROUPS, SSM_STATE, SSM_HPG * SSM_HEAD_DIM), F32)],
        compiler_params=_cparams(("parallel", "arbitrary")),
        name="ssd",
    )(main3, main3, aux3, conv_w, conv_b.reshape(1, ch), dtb, alog, d_x, norm_w.reshape(1, inner), expand)


def _qkv_kernel(x_ref, nw_ref, w_ref, *refs, tm, tn, dilations):
    n_grp = len(dilations)
    q_refs, k_refs, v_refs = refs[:n_grp], refs[n_grp:2 * n_grp], refs[2 * n_grp:3 * n_grp]
    h_ref, acc_ref, mid_ref = refs[3 * n_grp:]
    j = pl.program_id(2)
    per = ATT_KV_WIDTH // tn
    n_slabs = tn // LANES
    d_mid, d_far = dilations[1], dilations[2]
    fan = d_far // d_mid
    rows_mid, rows_far = tm // d_mid, tm // d_far

    def project_and_emit(targets, scale, first=False):
        if first:
            h = _rms(x_ref[...], nw_ref[...]).astype(BF16)
            h_ref[...] = h
        else:
            h = h_ref[...]
        acc = _dot(h, w_ref[...])
        if scale != 1.0:
            acc = acc * scale
        by_d = {d: o_ref for o_ref, d in targets}
        if 1 in by_d:
            by_d[1][0] = acc.astype(BF16)
        if d_mid not in by_d and d_far not in by_d:
            return
        for c in range(n_slabs):
            acc_ref[c] = acc[:, c * LANES:(c + 1) * LANES]
        for r in range(d_mid):
            for c in range(n_slabs):
                val = acc_ref[c, pl.ds(r, rows_mid, stride=d_mid), :]
                if d_mid in by_d:
                    by_d[d_mid][r, :, c * LANES:(c + 1) * LANES] = val.astype(BF16)
                if d_far in by_d:
                    mid_ref[c, r * rows_mid:(r + 1) * rows_mid, :] = val
        if d_far in by_d:
            for r in range(d_mid):
                for k in range(fan):
                    for c in range(n_slabs):
                        val = mid_ref[c, pl.ds(r * rows_mid + k, rows_far, stride=fan), :]
                        by_d[d_far][r + d_mid * k, :, c * LANES:(c + 1) * LANES] = val.astype(BF16)

    @pl.when(j == 0)
    def _():
        project_and_emit([(q_refs[0], dilations[0])], ATT_HEAD_DIM ** -0.5, first=True)

    for g, d in enumerate(dilations):
        @pl.when((j >= max(g * per, 1)) & (j < (g + 1) * per))
        def _(g=g, d=d):
            project_and_emit([(q_refs[g], d)], ATT_HEAD_DIM ** -0.5)

    @pl.when((j >= n_grp * per) & (j < (n_grp + 1) * per))
    def _():
        project_and_emit(list(zip(k_refs, dilations)), 1.0)

    @pl.when(j >= (n_grp + 1) * per)
    def _():
        project_and_emit(list(zip(v_refs, dilations)), 1.0)


def _qkv_proj(x3, nw, w):
    bsz, s, dm = x3.shape
    tm, tn = min(PROJ_TOKEN_TILE, s), QKV_COL_TILE
    per = ATT_KV_WIDTH // tn
    dilations = tuple(d for _, d in ATT_PATTERNS)
    n_grp = len(dilations)
    assert dilations[0] == 1 and dilations[2] % dilations[1] == 0

    def out_spec(d, j0):
        return pl.BlockSpec((None, d, tm // d, tn),
                            lambda b, i, j: (b, 0, i, jnp.clip(j - j0, 0, per - 1)))

    starts = [g * per for g in range(n_grp)] + [n_grp * per] * n_grp + [(n_grp + 1) * per] * n_grp
    out_specs = [out_spec(d, j0) for d, j0 in zip(dilations * 3, starts)]
    out_shape = [jax.ShapeDtypeStruct((bsz, d, s // d, ATT_KV_WIDTH), BF16) for d in dilations * 3]
    outs = pl.pallas_call(
        functools.partial(_qkv_kernel, tm=tm, tn=tn, dilations=dilations),
        grid=(bsz, s // tm, w.shape[1] // tn),
        in_specs=[
            pl.BlockSpec((None, tm, dm), lambda b, i, j: (b, i, 0)),
            pl.BlockSpec((1, dm), lambda b, i, j: (0, 0)),
            pl.BlockSpec((dm, tn), lambda b, i, j: (0, j)),
        ],
        out_specs=out_specs,
        out_shape=out_shape,
        scratch_shapes=[pltpu.VMEM((tm, dm), BF16), pltpu.VMEM((tn // LANES, tm, LANES), F32),
                        pltpu.VMEM((tn // LANES, tm, LANES), F32)],
        compiler_params=_cparams(("parallel", "parallel", "arbitrary")),
        name="qkv_proj",
    )(x3, nw.reshape(1, dm), w)
    return outs[:n_grp], outs[n_grp:2 * n_grp], outs[2 * n_grp:]


def _attn_kernel(q_ref, k_ref, v_ref, kp_ref, vp_ref, o_ref, st_ref, *, n_qblocks, span):
    blk, dh = ATT_BLOCK, ATT_HEAD_DIM
    n = pl.program_id(2)
    row = lax.broadcasted_iota(jnp.int32, (blk, 2 * blk), 0)
    col = lax.broadcasted_iota(jnp.int32, (blk, 2 * blk), 1)
    dist = row + blk - col
    mask = (dist >= 0) & (dist <= span)
    mask_first = mask & ((col >= blk) | (n > 0))
    lane = lax.broadcasted_iota(jnp.int32, (blk, LANES), 1)
    ones = jnp.ones((2 * blk, dh), BF16)

    for j in range(n_qblocks):
        rows = slice(j * blk, (j + 1) * blk)
        tile = jnp.zeros((blk, LANES), F32)
        for h in range(ATT_KV_HEADS):
            sl = slice(h * dh, (h + 1) * dh)
            q = q_ref[rows, sl]
            if j == 0:
                k2 = jnp.concatenate([kp_ref[:, sl], k_ref[rows, sl]], axis=0)
                v2 = jnp.concatenate([vp_ref[:, sl], v_ref[rows, sl]], axis=0)
            else:
                rows2 = slice((j - 1) * blk, (j + 1) * blk)
                k2 = k_ref[rows2, sl]
                v2 = v_ref[rows2, sl]
            s = jnp.where(mask_first if j == 0 else mask, _dot_nt(q, k2), -jnp.inf)
            m = jnp.max(s, axis=-1, keepdims=True)
            p = jnp.exp(s - m).astype(BF16)
            ov = _dot(p, jnp.concatenate([v2, ones], axis=1))
            l = ov[:, dh:]
            o_ref[rows, sl] = (ov[:, :dh] / l).astype(o_ref.dtype)
            tile = jnp.where(lane == h, m + jnp.log(l), tile)
        st_ref[rows, :] = tile


def _attn_group(q, k, v, span):
    bsz, d, m, w = q.shape
    rows = min(ATT_ROWS, m)
    qb_per_step = rows // ATT_BLOCK
    cur = pl.BlockSpec((None, None, rows, w), lambda b, r, n: (b, r, n, 0))
    prev = pl.BlockSpec((None, None, ATT_BLOCK, w), lambda b, r, n: (b, r, jnp.maximum(n * qb_per_step - 1, 0), 0))
    return pl.pallas_call(
        functools.partial(_attn_kernel, n_qblocks=qb_per_step, span=span),
        grid=(bsz, d, m // rows),
        in_specs=[cur, cur, cur, prev, prev],
        out_specs=[cur, pl.BlockSpec((None, None, rows, LANES), lambda b, r, n: (b, r, n, 0))],
        out_shape=[jax.ShapeDtypeStruct((bsz, d, m, w), BF16), jax.ShapeDtypeStruct((bsz, d, m, LANES), F32)],
        compiler_params=_cparams(("parallel", "parallel", "arbitrary")),
        name=f"attn_d{d}",
    )(q, k, v, k, v)


def _attn_out_kernel(*refs, tm, dilations):
    n_grp = len(dilations)
    o_refs, s_refs = refs[:n_grp], refs[n_grp:2 * n_grp]
    w_ref, npost_ref, x_ref, out_ref, on_ref, sn_ref, comb_ref = refs[2 * n_grp:]
    dh = ATT_HEAD_DIM

    for g, d in enumerate(dilations):
        rows = tm // d
        for r in range(d):
            if d == 1:
                sn_ref[g] = s_refs[g][r]
            else:
                sn_ref[g, pl.ds(r, rows, stride=d), :] = s_refs[g][r]
                for h in range(ATT_KV_HEADS):
                    on_ref[g, h, pl.ds(r, rows, stride=d), :] = o_refs[g][r, :, h * dh:(h + 1) * dh].astype(F32)

    lses = [sn_ref[g] for g in range(n_grp)]
    mx = functools.reduce(jnp.maximum, lses)
    es = [jnp.exp(v - mx) for v in lses]
    inv = 1.0 / functools.reduce(lambda a, b: a + b, es)
    wts = [e * inv for e in es]
    y = None
    for h in range(ATT_KV_HEADS):
        sl = slice(h * dh, (h + 1) * dh)
        acc = None
        for g, d in enumerate(dilations):
            og = o_refs[g][0, :, sl].astype(F32) if d == 1 else on_ref[g, h]
            term = wts[g][:, h:h + 1] * og
            acc = term if acc is None else acc + term
        comb_ref[:, sl] = acc.astype(BF16)
        if h % 2 == 1:
            pair = slice((h - 1) * dh, (h + 1) * dh)
            part = _dot(comb_ref[:, pair], w_ref[pair, :])
            y = part if y is None else y + part
    out_ref[...] = x_ref[...] + _rms(y, npost_ref[...])


def _attn_out(os, sts, w, npost, x3):
    bsz, s, dm = x3.shape
    tm = min(ATTN_OUT_TILE, s)
    kw = ATT_KV_WIDTH
    dilations = tuple(o.shape[1] for o in os)
    n_grp = len(dilations)
    grp_blk = lambda d, width: pl.BlockSpec((None, d, tm // d, width), lambda b, i: (b, 0, i, 0))
    row_blk = pl.BlockSpec((None, tm, dm), lambda b, i: (b, i, 0))
    return pl.pallas_call(
        functools.partial(_attn_out_kernel, tm=tm, dilations=dilations),
        grid=(bsz, s // tm),
        in_specs=[grp_blk(d, kw) for d in dilations] + [grp_blk(d, LANES) for d in dilations] + [
            pl.BlockSpec((kw, dm), lambda b, i: (0, 0)),
            pl.BlockSpec((1, dm), lambda b, i: (0, 0)),
            row_blk,
        ],
        out_specs=row_blk,
        out_shape=jax.ShapeDtypeStruct((bsz, s, dm), F32),
        scratch_shapes=[pltpu.VMEM((n_grp, ATT_KV_HEADS, tm, LANES), F32),
                        pltpu.VMEM((n_grp, tm, LANES), F32),
                        pltpu.VMEM((tm, kw), BF16)],
        compiler_params=_cparams(("parallel", "parallel")),
        name="attn_out",
    )(*os, *sts, w, npost.reshape(1, dm), x3)


def _mixer_ab(h2, bsz, s, nw_pre, nw_post, w_in, w_out, hgrn_lb, layer_idx, hg_norm_w,
              conv_w, conv_b, dt_bias, a_log, d_skip, ssm_norm_w):
    dt_col = 4 * HG_WIDTH + SSM_INNER + SSM_CONV_CH
    w_dt = jnp.pad(w_in[:, dt_col:], ((0, 0), (0, LANES - SSM_HEADS))).astype(BF16)
    main, aux = _ab_proj(h2, nw_pre, w_in.astype(BF16), w_dt)
    main3 = main.reshape(bsz, s, AB_MAIN_COLS)
    aux3 = aux.reshape(bsz, s, AB_AUX_COLS)
    o_a = _hgrn2(main3, aux3, hgrn_lb, hg_norm_w, layer_idx)
    o_b = _ssd(main3, aux3, conv_w, conv_b, dt_bias, a_log, d_skip, ssm_norm_w)
    t = bsz * s
    return _outproj_res([o_a.reshape(t, HG_WIDTH), o_b.reshape(t, SSM_INNER)], w_out.astype(BF16), nw_post, h2)


def _mixer_c(h2, bsz, s, nw_pre, nw_post, w_in, w_out):
    dm = h2.shape[1]
    x3 = h2.reshape(bsz, s, dm)
    qs, ks, vs = _qkv_proj(x3, nw_pre, w_in.astype(BF16))
    os, sts = [], []
    for g, (window, dilation) in enumerate(ATT_PATTERNS):
        o, st = _attn_group(qs[g], ks[g], vs[g], window // dilation)
        os.append(o)
        sts.append(st)
    return _attn_out(os, sts, w_out.astype(BF16), nw_post, x3).reshape(bsz * s, dm)


def kernel(x, norm_pre, norm_post, ffn_w_gate, ffn_w_up, ffn_w_down, ab_w_in, ab_w_out, hgrn_lb, hgrn_norm_w,
           ssm_conv_w, ssm_conv_b, ssm_dt_bias, ssm_A_log, ssm_D, ssm_norm_w, att_w_in, att_w_out):
    bsz, s, d = x.shape
    depth = norm_pre.shape[0]
    h2 = x.reshape(bsz * s, d)

    def chunk_major(w):
        return w.astype(BF16).reshape(*w.shape[:3], w.shape[3] // FF_TILE, FF_TILE).transpose(0, 1, 3, 2, 4)

    wg, wu, wd = chunk_major(ffn_w_gate), chunk_major(ffn_w_up), ffn_w_down.astype(BF16)

    def ffn(h2, layer, k):
        return _ffn(h2, norm_pre[layer, 2 * k], norm_post[layer, 2 * k], wg, wu, wd, layer, k)

    for layer in range(depth):
        h2 = ffn(h2, layer, 0)
        if layer % 2 == 0:
            e = layer // 2
            h2 = _mixer_ab(h2, bsz, s, norm_pre[layer, 1], norm_post[layer, 1], ab_w_in[e], ab_w_out[e],
                           hgrn_lb, e, hgrn_norm_w[e], ssm_conv_w[e], ssm_conv_b[e], ssm_dt_bias[e],
                           ssm_A_log[e], ssm_D[e], ssm_norm_w[e])
        else:
            o_idx = layer // 2
            h2 = _mixer_c(h2, bsz, s, norm_pre[layer, 1], norm_post[layer, 1], att_w_in[o_idx], att_w_out[o_idx])
        h2 = ffn(h2, layer, 1)
    return h2.reshape(bsz, s, d)
```

```python
import functools

import jax
import jax.numpy as jnp
from jax import lax
from jax.experimental import pallas as pl
from jax.experimental.pallas import tpu as pltpu

F32 = jnp.float32
BF16 = jnp.bfloat16
EPS = 1e-6
FFN_RES = 0.5

HG_HEADS = 8
HG_DK = 128
HG_WIDTH = HG_HEADS * HG_DK
HG_CHUNK = 64
HG_INTRA_ROWS = 128

SSM_INNER = 1024
SSM_HEAD_DIM = 64
SSM_HEADS = SSM_INNER // SSM_HEAD_DIM
SSM_GROUPS = 4
SSM_HPG = SSM_HEADS // SSM_GROUPS
SSM_STATE = 128
SSM_CONV = 4
SSM_CHUNK = 128
SSM_CONV_CH = SSM_INNER + 2 * SSM_GROUPS * SSM_STATE

ATT_PATTERNS = ((128, 1), (512, 4), (2048, 16))
ATT_GROUPS = 3
ATT_KV_HEADS = 16
ATT_HEAD_DIM = 128
ATT_BLOCK = 128
ATT_Q_WIDTH = ATT_GROUPS * ATT_KV_HEADS * ATT_HEAD_DIM
ATT_KV_WIDTH = ATT_KV_HEADS * ATT_HEAD_DIM
ATT_IN = ATT_Q_WIDTH + 2 * ATT_KV_WIDTH

LANES = 128
SUBLANES = 8
VMEM_LIMIT_BYTES = 56 * 1024 * 1024
FFN_VMEM_LIMIT_BYTES = 60 * 1024 * 1024

TOKEN_TILE = 512
FFN_TOKEN_TILE = 1024
PROJ_TOKEN_TILE = 1024
FF_TILE = 512
PROJ_COL_TILE = 1024
MIXER_ROWS = 512
ATT_ROWS = 512
ATTN_OUT_TILE = 256

AB_MAIN_COLS = 4 * HG_WIDTH + SSM_CONV_CH
AB_AUX_COLS = HG_WIDTH + LANES


def _cparams(semantics, vmem_limit_bytes=VMEM_LIMIT_BYTES):
    return pltpu.CompilerParams(dimension_semantics=semantics, vmem_limit_bytes=vmem_limit_bytes)


def _rms(x, w):
    ms = jnp.mean(x * x, axis=-1, keepdims=True)
    return x * lax.rsqrt(ms + EPS) * w


def _silu(x):
    return x * jax.nn.sigmoid(x)


def _dot(a, b):
    return jnp.dot(a, b, preferred_element_type=F32)


def _dot_nt(a, b):
    return lax.dot_general(a, b, (((1,), (1,)), ((), ())), preferred_element_type=F32)


def _dot_tn(a, b):
    return lax.dot_general(a, b, (((0,), (0,)), ((), ())), preferred_element_type=F32)


def _split3(x):
    hi = x.astype(BF16)
    r = x - hi.astype(F32)
    mid = r.astype(BF16)
    lo = (r - mid.astype(F32)).astype(BF16)
    return hi, mid, lo


def _sel_dot(sel, x):
    hi, mid, lo = _split3(x)
    return _dot(sel, hi) + _dot(sel, mid) + _dot(sel, lo)


def _dot_sel(x, sel):
    hi, mid, lo = _split3(x)
    return _dot(hi, sel) + _dot(mid, sel) + _dot(lo, sel)


def _dot_tn_sel(x, sel):
    hi, mid, lo = _split3(x)
    return _dot_tn(hi, sel) + _dot_tn(mid, sel) + _dot_tn(lo, sel)


def _ffn_kernel(x_ref, npre_ref, npost_ref, wg_ref, wu_ref, wd_ref, o_ref, h_ref):
    j = pl.program_id(1)
    last = pl.num_programs(1) - 1

    def chunk_product(h):
        g = _dot(h, wg_ref[...])
        u = _dot(h, wu_ref[...])
        return _dot((_silu(g) * u).astype(BF16), wd_ref[...])

    @pl.when(j == 0)
    def _():
        h = _rms(x_ref[...], npre_ref[...]).astype(BF16)
        h_ref[...] = h
        o_ref[...] = chunk_product(h)

    @pl.when((j > 0) & (j < last))
    def _():
        o_ref[...] += chunk_product(h_ref[...])

    @pl.when(j == last)
    def _():
        y = o_ref[...] + chunk_product(h_ref[...])
        o_ref[...] = x_ref[...] + _rms(y, FFN_RES * npost_ref[...])


def _ffn(x2, npre, npost, wg, wu, wd, layer, k):
    t, d = x2.shape
    f = wg.shape[-1]
    tm, tf = min(FFN_TOKEN_TILE, t), FF_TILE
    assert f // tf >= 2
    return pl.pallas_call(
        _ffn_kernel,
        grid=(t // tm, f // tf),
        in_specs=[
            pl.BlockSpec((tm, d), lambda i, j: (i, 0)),
            pl.BlockSpec((1, d), lambda i, j: (0, 0)),
            pl.BlockSpec((1, d), lambda i, j: (0, 0)),
            pl.BlockSpec((None, None, d, tf), lambda i, j: (layer, k, 0, j)),
            pl.BlockSpec((None, None, d, tf), lambda i, j: (layer, k, 0, j)),
            pl.BlockSpec((None, None, tf, d), lambda i, j: (layer, k, j, 0)),
        ],
        out_specs=pl.BlockSpec((tm, d), lambda i, j: (i, 0)),
        out_shape=jax.ShapeDtypeStruct((t, d), F32),
        scratch_shapes=[pltpu.VMEM((tm, d), BF16)],
        compiler_params=_cparams(("parallel", "arbitrary"), FFN_VMEM_LIMIT_BYTES),
        name="ffn",
    )(x2, npre.reshape(1, d), npost.reshape(1, d), wg, wu, wd)


def _ab_proj_kernel(x_ref, nw_ref, w_ref, whf_ref, wdt_ref, o_ref, oaux_ref, h_ref):
    j = pl.program_id(1)

    @pl.when(j == 0)
    def _():
        h = _rms(x_ref[...], nw_ref[...]).astype(BF16)
        h_ref[...] = h
        oaux_ref[:, :HG_WIDTH] = _dot(h, whf_ref[...])
        oaux_ref[:, HG_WIDTH:] = _dot(h, wdt_ref[...])
        o_ref[...] = _dot(h, w_ref[...]).astype(o_ref.dtype)

    @pl.when(j > 0)
    def _():
        o_ref[...] = _dot(h_ref[...], w_ref[...]).astype(o_ref.dtype)


def _ab_proj(x2, nw, w_in, w_dt):
    t, d = x2.shape
    tm, tn = min(PROJ_TOKEN_TILE, t), PROJ_COL_TILE
    assert tn == HG_WIDTH
    n_main = AB_MAIN_COLS // tn
    return pl.pallas_call(
        _ab_proj_kernel,
        grid=(t // tm, n_main),
        in_specs=[
            pl.BlockSpec((tm, d), lambda i, j: (i, 0)),
            pl.BlockSpec((1, d), lambda i, j: (0, 0)),
            pl.BlockSpec((d, tn), lambda i, j: (0, jnp.where(j >= 1, j + 1, j))),
            pl.BlockSpec((d, tn), lambda i, j: (0, 1)),
            pl.BlockSpec((d, LANES), lambda i, j: (0, 0)),
        ],
        out_specs=[pl.BlockSpec((tm, tn), lambda i, j: (i, j)),
                   pl.BlockSpec((tm, AB_AUX_COLS), lambda i, j: (i, 0))],
        out_shape=[jax.ShapeDtypeStruct((t, AB_MAIN_COLS), BF16), jax.ShapeDtypeStruct((t, AB_AUX_COLS), F32)],
        scratch_shapes=[pltpu.VMEM((tm, d), BF16)],
        compiler_params=_cparams(("parallel", "arbitrary")),
        name="ab_proj",
    )(x2, nw.reshape(1, d), w_in, w_in, w_dt)


def _outproj_kernel(*refs, n_parts):
    parts = refs[:n_parts]
    ws = refs[n_parts:2 * n_parts]
    npost_ref, x_ref, o_ref = refs[2 * n_parts:]
    y = _dot(parts[0][...], ws[0][...])
    for p_ref, w_ref in zip(parts[1:], ws[1:]):
        y = y + _dot(p_ref[...], w_ref[...])
    o_ref[...] = x_ref[...] + _rms(y, npost_ref[...])


def _outproj_res(parts, w, npost, x2):
    t, d = x2.shape
    tm = min(TOKEN_TILE, t)
    n_parts = len(parts)
    kp = w.shape[0] // n_parts
    in_specs = [pl.BlockSpec((tm, kp), lambda i: (i, 0)) for _ in parts]
    in_specs += [pl.BlockSpec((kp, d), functools.partial(lambda i, p: (p, 0), p=p)) for p in range(n_parts)]
    in_specs += [pl.BlockSpec((1, d), lambda i: (0, 0)), pl.BlockSpec((tm, d), lambda i: (i, 0))]
    return pl.pallas_call(
        functools.partial(_outproj_kernel, n_parts=n_parts),
        grid=(t // tm,),
        in_specs=in_specs,
        out_specs=pl.BlockSpec((tm, d), lambda i: (i, 0)),
        out_shape=jax.ShapeDtypeStruct((t, d), F32),
        compiler_params=_cparams(("parallel",)),
        name="outproj_res",
    )(*parts, *([w] * n_parts), npost.reshape(1, d), x2)


def _hgrn_kernel(q_ref, f_ref, i_ref, g_ref, lb_ref, nw_ref, o_ref, st_ref, *, layer_idx, n_chunks):
    c_len = HG_CHUNK

    @pl.when(pl.program_id(1) == 0)
    def _():
        st_ref[...] = jnp.zeros_like(st_ref)

    lbs = lb_ref[...]
    ex = jnp.exp(lbs - jnp.max(lbs, axis=0, keepdims=True))
    sm = ex / jnp.sum(ex, axis=0, keepdims=True)
    lb = jnp.sum(sm[:layer_idx + 1], axis=0, keepdims=True)

    span = HG_INTRA_ROWS
    rows_total = n_chunks * c_len
    per_span = span // c_len
    row = lax.broadcasted_iota(jnp.int32, (span, span), 0)
    col = lax.broadcasted_iota(jnp.int32, (span, span), 1)
    same_chunk = (row // c_len) == (col // c_len)
    causal = (col <= row) & same_chunk
    tril = jnp.where(causal, 1.0, 0.0).astype(BF16)

    q_intra, k_intra, q_inter, k_state, decays = [], [], [], [], []
    for s0 in range(0, rows_total, span):
        rows = slice(s0, s0 + span)
        f = lb + (1.0 - lb) * jax.nn.sigmoid(f_ref[0, rows, :])
        log_f = jnp.log(f)
        k = 1.0 - f
        cum = _sel_dot(tril, log_f)
        cum3 = cum.reshape(per_span, c_len, HG_WIDTH)
        b_mid = jnp.broadcast_to(cum3[:, c_len // 2:c_len // 2 + 1, :], cum3.shape).reshape(span, HG_WIDTH)
        last = cum3[:, c_len - 1:c_len, :]
        b_last = jnp.broadcast_to(last, cum3.shape).reshape(span, HG_WIDTH)
        q = q_ref[0, rows, :].astype(F32)
        q_intra.append((q * jnp.exp(cum - b_mid)).astype(BF16))
        k_intra.append((k * jnp.exp(b_mid - cum)).astype(BF16))
        q_inter.append((q * jnp.exp(cum)).astype(BF16))
        k_state.append((k * jnp.exp(b_last - cum)).astype(BF16))
        decays.extend(jnp.exp(last[c]) for c in range(per_span))

    for h in range(HG_HEADS):
        sl = slice(h * HG_DK, (h + 1) * HG_DK)
        state_t = st_ref[h]
        outs = []
        for si, s0 in enumerate(range(0, rows_total, span)):
            v = i_ref[0, s0:s0 + span, sl]
            scores = jnp.where(causal, _dot_nt(q_intra[si][:, sl], k_intra[si][:, sl]), 0.0)
            o_intra = _dot(scores.astype(BF16), v)
            inter = []
            for c in range(per_span):
                cr = slice(c * c_len, (c + 1) * c_len)
                inter.append(_dot_nt(q_inter[si][cr, sl], state_t.astype(BF16)))
                state_t = state_t * decays[si * per_span + c][:, sl] + _dot_tn(v[cr, :], k_state[si][cr, sl])
            outs.append(o_intra + jnp.concatenate(inter, axis=0))
        st_ref[h] = state_t
        o = jnp.concatenate(outs, axis=0)
        ms = jnp.mean(o * o, axis=-1, keepdims=True)
        gate = _silu(g_ref[0, :, sl].astype(F32))
        o_ref[0, :, sl] = (o * lax.rsqrt(ms + EPS) * nw_ref[:, sl] * gate).astype(o_ref.dtype)


def _hgrn2(main3, aux3, hgrn_lb, norm_w, layer_idx):
    bsz, s, _ = main3.shape
    rows = min(MIXER_ROWS, s)
    w = HG_WIDTH
    blk = lambda cb: pl.BlockSpec((1, rows, w), functools.partial(lambda b, n, cb: (b, n, cb), cb=cb))
    n_lb = hgrn_lb.shape[0]
    return pl.pallas_call(
        functools.partial(_hgrn_kernel, layer_idx=layer_idx, n_chunks=rows // HG_CHUNK),
        grid=(bsz, s // rows),
        in_specs=[
            blk(0),
            blk(0),
            blk(1),
            blk(2),
            pl.BlockSpec((n_lb, w), lambda b, n: (0, 0)),
            pl.BlockSpec((1, w), lambda b, n: (0, 0)),
        ],
        out_specs=pl.BlockSpec((1, rows, w), lambda b, n: (b, n, 0)),
        out_shape=jax.ShapeDtypeStruct((bsz, s, w), BF16),
        scratch_shapes=[pltpu.VMEM((HG_HEADS, HG_DK, HG_DK), F32)],
        compiler_params=_cparams(("parallel", "arbitrary")),
        name="hgrn2",
    )(main3, aux3, main3, main3, hgrn_lb, norm_w.reshape(1, w))


def _softplus(x):
    return jnp.maximum(x, 0.0) + jnp.log1p(jnp.exp(-jnp.abs(x)))


def _ssd_kernel(z_ref, xbc_ref, dt_ref, cw_ref, cb_ref, dtb_ref, alog_ref, d_ref, nw_ref, e_ref,
                o_ref, xp_ref, ht_ref, *, n_chunks):
    c_len, n_st, p_dim = SSM_CHUNK, SSM_STATE, SSM_HEAD_DIM
    halo = SUBLANES
    gw = SSM_HPG * p_dim
    rows_total = n_chunks * c_len

    @pl.when(pl.program_id(1) == 0)
    def _():
        xp_ref[0:halo, :] = jnp.zeros((halo, SSM_CONV_CH), F32)
        ht_ref[...] = jnp.zeros_like(ht_ref)

    @pl.when(pl.program_id(1) > 0)
    def _():
        xp_ref[0:halo, :] = xp_ref[rows_total:rows_total + halo, :]

    xp_ref[halo:, :] = xbc_ref[0].astype(F32)

    row = lax.broadcasted_iota(jnp.int32, (c_len, c_len), 0)
    col = lax.broadcasted_iota(jnp.int32, (c_len, c_len), 1)
    causal = col <= row
    tril = jnp.where(causal, 1.0, 0.0).astype(BF16)
    triu = jnp.where(row <= col, 1.0, 0.0).astype(BF16)
    lane = lax.broadcasted_iota(jnp.int32, (c_len, LANES), 1)
    first_half = lane < p_dim
    expand = e_ref[...]
    neg_a = -jnp.exp(alog_ref[...])
    cw = cw_ref[...]
    cb = cb_ref[...]

    def chunk(c, carry):
        r0 = pl.multiple_of(c * c_len, c_len)
        rows = pl.ds(r0, c_len)
        full = xp_ref[pl.ds(r0, c_len + halo), :]
        conv = cb
        for j in range(SSM_CONV):
            off = halo - (SSM_CONV - 1) + j
            conv = conv + cw[j:j + 1, :] * full[off:off + c_len, :]
        act = _silu(conv)
        xs = act[:, :SSM_INNER]
        b_all = act[:, SSM_INNER:SSM_INNER + SSM_GROUPS * n_st].astype(BF16)
        c_all = act[:, SSM_INNER + SSM_GROUPS * n_st:].astype(BF16)

        dt_h = _softplus(dt_ref[0, rows, :] + dtb_ref[...])
        dta_h = dt_h * neg_a
        acs_h = _sel_dot(tril, dta_h)
        acs_t = _dot_tn_sel(dta_h, triu)
        dt_x = _dot_sel(dt_h, expand)
        acs_x = _dot_sel(acs_h, expand)
        a_last = acs_x[c_len - 1:c_len, :]
        xdt = xs * dt_x
        dec_x = (jnp.exp(a_last - acs_x) * xdt).astype(BF16)
        e_acs = jnp.exp(acs_x)
        chunk_decay = jnp.exp(a_last)

        ys = []
        for g in range(SSM_GROUPS):
            bg = b_all[:, g * n_st:(g + 1) * n_st]
            cg = c_all[:, g * n_st:(g + 1) * n_st]
            cbm = _dot_nt(cg, bg)
            y_pairs = []
            for pair in range(SSM_HPG // 2):
                lo = g * gw + pair * LANES
                x_pair = xdt[:, lo:lo + LANES]
                halves = (jnp.where(first_half, x_pair, 0.0).astype(BF16),
                          jnp.where(first_half, 0.0, x_pair).astype(BF16))
                y_pair = None
                for sub in range(2):
                    hd = g * SSM_HPG + pair * 2 + sub
                    seg = acs_h[:, hd:hd + 1] - acs_t[hd:hd + 1, :]
                    decay_ls = jnp.exp(jnp.where(causal, seg, -jnp.inf))
                    term = _dot((cbm * decay_ls).astype(BF16), halves[sub])
                    y_pair = term if y_pair is None else y_pair + term
                y_pairs.append(y_pair)
            y_diag = jnp.concatenate(y_pairs, axis=-1)
            gs = slice(g * gw, (g + 1) * gw)
            h_t = ht_ref[g]
            y_off = _dot(cg, h_t.astype(BF16)) * e_acs[:, gs]
            ht_ref[g] = h_t * chunk_decay[:, gs] + _dot_tn(bg, dec_x[:, gs])
            ys.append(y_diag + y_off)
        y = jnp.concatenate(ys, axis=-1) + xs * d_ref[...]
        y = y * _silu(z_ref[0, rows, :].astype(F32))
        normed = []
        for g in range(SSM_GROUPS):
            yg = y[:, g * gw:(g + 1) * gw]
            ms = jnp.mean(yg * yg, axis=-1, keepdims=True)
            normed.append(yg * lax.rsqrt(ms + EPS))
        o_ref[0, rows, :] = (jnp.concatenate(normed, axis=-1) * nw_ref[...]).astype(o_ref.dtype)
        return carry

    lax.fori_loop(0, n_chunks, chunk, 0)


def _ssd(main3, aux3, conv_w, conv_b, dt_bias, a_log, d_skip, norm_w):
    bsz, s, _ = main3.shape
    rows = min(MIXER_ROWS, s)
    inner, ch = SSM_INNER, SSM_CONV_CH
    pad = LANES - SSM_HEADS
    dtb = jnp.pad(dt_bias, (0, pad)).reshape(1, LANES)
    alog = jnp.pad(a_log, (0, pad)).reshape(1, LANES)
    d_x = jnp.repeat(d_skip, SSM_HEAD_DIM).reshape(1, inner)
    head_of_lane = jnp.arange(inner) // SSM_HEAD_DIM
    expand = (jnp.arange(LANES)[:, None] == head_of_lane[None, :]).astype(BF16)
    const = lambda shape: pl.BlockSpec(shape, lambda b, n: (0, 0))
    return pl.pallas_call(
        functools.partial(_ssd_kernel, n_chunks=rows // SSM_CHUNK),
        grid=(bsz, s // rows),
        in_specs=[
            pl.BlockSpec((1, rows, inner), lambda b, n: (b, n, 3)),
            pl.BlockSpec((1, rows, ch), lambda b, n: (b, n, 2)),
            pl.BlockSpec((1, rows, LANES), lambda b, n: (b, n, HG_WIDTH // LANES)),
            const((SSM_CONV, ch)), const((1, ch)), const((1, LANES)), const((1, LANES)),
            const((1, inner)), const((1, inner)), const((LANES, inner)),
        ],
        out_specs=pl.BlockSpec((1, rows, inner), lambda b, n: (b, n, 0)),
        out_shape=jax.ShapeDtypeStruct((bsz, s, inner), BF16),
        scratch_shapes=[pltpu.VMEM((rows + SUBLANES, ch), F32),
                        pltpu.VMEM((SSM_GROUPS, SSM_STATE, SSM_HPG * SSM_HEAD_DIM), F32)],
        compiler_params=_cparams(("parallel", "arbitrary")),
        name="ssd",
    )(main3, main3, aux3, conv_w, conv_b.reshape(1, ch), dtb, alog, d_x, norm_w.reshape(1, inner), expand)


def _qkv_kernel(x_ref, nw_ref, w_ref, *refs, tm, tn, dilations):
    n_grp = len(dilations)
    out_refs = refs[:n_grp]
    h_ref, acc_ref, mid_ref = refs[n_grp:]
    j = pl.program_id(2)
    per = ATT_KV_WIDTH // tn
    n_slabs = tn // LANES
    d_mid, d_far = dilations[1], dilations[2]
    fan = d_far // d_mid
    rows_mid, rows_far = tm // d_mid, tm // d_far

    def project_and_emit(targets, scale, first=False):
        if first:
            h = _rms(x_ref[...], nw_ref[...]).astype(BF16)
            h_ref[...] = h
        else:
            h = h_ref[...]
        acc = _dot(h, w_ref[...])
        if scale != 1.0:
            acc = acc * scale
        by_d = {d: o_ref for o_ref, d in targets}
        if 1 in by_d:
            by_d[1][0] = acc.astype(BF16)
        if d_mid not in by_d and d_far not in by_d:
            return
        for c in range(n_slabs):
            acc_ref[c] = acc[:, c * LANES:(c + 1) * LANES]
        for r in range(d_mid):
            for c in range(n_slabs):
                val = acc_ref[c, pl.ds(r, rows_mid, stride=d_mid), :]
                if d_mid in by_d:
                    by_d[d_mid][r, :, c * LANES:(c + 1) * LANES] = val.astype(BF16)
                if d_far in by_d:
                    mid_ref[c, r * rows_mid:(r + 1) * rows_mid, :] = val
        if d_far in by_d:
            for r in range(d_mid):
                for k in range(fan):
                    for c in range(n_slabs):
                        val = mid_ref[c, pl.ds(r * rows_mid + k, rows_far, stride=fan), :]
                        by_d[d_far][r + d_mid * k, :, c * LANES:(c + 1) * LANES] = val.astype(BF16)

    @pl.when(j == 0)
    def _():
        project_and_emit([(out_refs[0], dilations[0])], ATT_HEAD_DIM ** -0.5, first=True)

    for g, d in enumerate(dilations):
        @pl.when((j >= max(g * per, 1)) & (j < (g + 1) * per))
        def _(g=g, d=d):
            project_and_emit([(out_refs[g], d)], ATT_HEAD_DIM ** -0.5)

    @pl.when((j >= n_grp * per) & (j < (n_grp + 1) * per))
    def _():
        project_and_emit(list(zip(out_refs, dilations)), 1.0)

    @pl.when(j >= (n_grp + 1) * per)
    def _():
        project_and_emit(list(zip(out_refs, dilations)), 1.0)


def _qkv_proj(x3, nw, w):
    bsz, s, dm = x3.shape
    tm, tn = min(PROJ_TOKEN_TILE, s), PROJ_COL_TILE
    per = ATT_KV_WIDTH // tn
    dilations = tuple(d for _, d in ATT_PATTERNS)
    n_grp = len(dilations)
    assert dilations[0] == 1 and dilations[2] % dilations[1] == 0

    def out_spec(g, d):
        def col(j):
            return jnp.where(j < n_grp * per, jnp.clip(j - g * per, 0, per - 1), j - (n_grp - 1) * per)
        return pl.BlockSpec((None, d, tm // d, tn), lambda b, i, j: (b, 0, i, col(j)))

    out_specs = [out_spec(g, d) for g, d in enumerate(dilations)]
    out_shape = [jax.ShapeDtypeStruct((bsz, d, s // d, 3 * ATT_KV_WIDTH), BF16) for d in dilations]
    outs = pl.pallas_call(
        functools.partial(_qkv_kernel, tm=tm, tn=tn, dilations=dilations),
        grid=(bsz, s // tm, w.shape[1] // tn),
        in_specs=[
            pl.BlockSpec((None, tm, dm), lambda b, i, j: (b, i, 0)),
            pl.BlockSpec((1, dm), lambda b, i, j: (0, 0)),
            pl.BlockSpec((dm, tn), lambda b, i, j: (0, j)),
        ],
        out_specs=out_specs,
        out_shape=out_shape,
        scratch_shapes=[pltpu.VMEM((tm, dm), BF16), pltpu.VMEM((tn // LANES, tm, LANES), F32),
                        pltpu.VMEM((tn // LANES, tm, LANES), F32)],
        compiler_params=_cparams(("parallel", "parallel", "arbitrary")),
        name="qkv_proj",
    )(x3, nw.reshape(1, dm), w)
    return outs


def _attn_kernel(q_ref, k_ref, v_ref, kp_ref, vp_ref, o_ref, st_ref, *, n_qblocks, span):
    blk, dh = ATT_BLOCK, ATT_HEAD_DIM
    n = pl.program_id(2)
    row = lax.broadcasted_iota(jnp.int32, (blk, 2 * blk), 0)
    col = lax.broadcasted_iota(jnp.int32, (blk, 2 * blk), 1)
    dist = row + blk - col
    mask = (dist >= 0) & (dist <= span)
    mask_first = mask & ((col >= blk) | (n > 0))
    lane = lax.broadcasted_iota(jnp.int32, (blk, LANES), 1)
    ones = jnp.ones((2 * blk, dh), BF16)

    for j in range(n_qblocks):
        rows = slice(j * blk, (j + 1) * blk)
        tile = jnp.zeros((blk, LANES), F32)
        for h in range(ATT_KV_HEADS):
            sl = slice(h * dh, (h + 1) * dh)
            q = q_ref[rows, sl]
            if j == 0:
                k2 = jnp.concatenate([kp_ref[:, sl], k_ref[rows, sl]], axis=0)
                v2 = jnp.concatenate([vp_ref[:, sl], v_ref[rows, sl]], axis=0)
            else:
                rows2 = slice((j - 1) * blk, (j + 1) * blk)
                k2 = k_ref[rows2, sl]
                v2 = v_ref[rows2, sl]
            s = jnp.where(mask_first if j == 0 else mask, _dot_nt(q, k2), -jnp.inf)
            m = jnp.max(s, axis=-1, keepdims=True)
            p = jnp.exp(s - m).astype(BF16)
            ov = _dot(p, jnp.concatenate([v2, ones], axis=1))
            l = ov[:, dh:]
            o_ref[rows, sl] = (ov[:, :dh] / l).astype(o_ref.dtype)
            tile = jnp.where(lane == h, m + jnp.log(l), tile)
        st_ref[rows, :] = tile


def _attn_group(qkv, span):
    bsz, d, m, _ = qkv.shape
    w = ATT_KV_WIDTH
    rows = min(ATT_ROWS, m)
    qb_per_step = rows // ATT_BLOCK
    cur = lambda part: pl.BlockSpec((None, None, rows, w), lambda b, r, n: (b, r, n, part))
    prev = lambda part: pl.BlockSpec((None, None, ATT_BLOCK, w),
                                     lambda b, r, n: (b, r, jnp.maximum(n * qb_per_step - 1, 0), part))
    return pl.pallas_call(
        functools.partial(_attn_kernel, n_qblocks=qb_per_step, span=span),
        grid=(bsz, d, m // rows),
        in_specs=[cur(0), cur(1), cur(2), prev(1), prev(2)],
        out_specs=[cur(0), pl.BlockSpec((None, None, rows, LANES), lambda b, r, n: (b, r, n, 0))],
        out_shape=[jax.ShapeDtypeStruct((bsz, d, m, w), BF16), jax.ShapeDtypeStruct((bsz, d, m, LANES), F32)],
        compiler_params=_cparams(("parallel", "parallel", "arbitrary")),
        name=f"attn_d{d}",
    )(qkv, qkv, qkv, qkv, qkv)


def _attn_out_kernel(*refs, tm, dilations):
    n_grp = len(dilations)
    o_refs, s_refs = refs[:n_grp], refs[n_grp:2 * n_grp]
    w_ref, npost_ref, x_ref, out_ref, on_ref, sn_ref, comb_ref = refs[2 * n_grp:]
    dh = ATT_HEAD_DIM

    for g, d in enumerate(dilations):
        rows = tm // d
        for r in range(d):
            if d == 1:
                sn_ref[g] = s_refs[g][r]
            else:
                sn_ref[g, pl.ds(r, rows, stride=d), :] = s_refs[g][r]
                for h in range(ATT_KV_HEADS):
                    on_ref[g, h, pl.ds(r, rows, stride=d), :] = o_refs[g][r, :, h * dh:(h + 1) * dh].astype(F32)

    lses = [sn_ref[g] for g in range(n_grp)]
    mx = functools.reduce(jnp.maximum, lses)
    es = [jnp.exp(v - mx) for v in lses]
    inv = 1.0 / functools.reduce(lambda a, b: a + b, es)
    wts = [e * inv for e in es]
    y = None
    for h in range(ATT_KV_HEADS):
        sl = slice(h * dh, (h + 1) * dh)
        acc = None
        for g, d in enumerate(dilations):
            og = o_refs[g][0, :, sl].astype(F32) if d == 1 else on_ref[g, h]
            term = wts[g][:, h:h + 1] * og
            acc = term if acc is None else acc + term
        comb_ref[:, sl] = acc.astype(BF16)
        if h % 2 == 1:
            pair = slice((h - 1) * dh, (h + 1) * dh)
            part = _dot(comb_ref[:, pair], w_ref[pair, :])
            y = part if y is None else y + part
    out_ref[...] = x_ref[...] + _rms(y, npost_ref[...])


def _attn_out(os, sts, w, npost, x3):
    bsz, s, dm = x3.shape
    tm = min(ATTN_OUT_TILE, s)
    kw = ATT_KV_WIDTH
    dilations = tuple(o.shape[1] for o in os)
    n_grp = len(dilations)
    grp_blk = lambda d, width: pl.BlockSpec((None, d, tm // d, width), lambda b, i: (b, 0, i, 0))
    row_blk = pl.BlockSpec((None, tm, dm), lambda b, i: (b, i, 0))
    return pl.pallas_call(
        functools.partial(_attn_out_kernel, tm=tm, dilations=dilations),
        grid=(bsz, s // tm),
        in_specs=[grp_blk(d, kw) for d in dilations] + [grp_blk(d, LANES) for d in dilations] + [
            pl.BlockSpec((kw, dm), lambda b, i: (0, 0)),
            pl.BlockSpec((1, dm), lambda b, i: (0, 0)),
            row_blk,
        ],
        out_specs=row_blk,
        out_shape=jax.ShapeDtypeStruct((bsz, s, dm), F32),
        scratch_shapes=[pltpu.VMEM((n_grp, ATT_KV_HEADS, tm, LANES), F32),
                        pltpu.VMEM((n_grp, tm, LANES), F32),
                        pltpu.VMEM((tm, kw), BF16)],
        compiler_params=_cparams(("parallel", "parallel")),
        name="attn_out",
    )(*os, *sts, w, npost.reshape(1, dm), x3)


def _mixer_ab(h2, bsz, s, nw_pre, nw_post, w_in, w_out, hgrn_lb, layer_idx, hg_norm_w,
              conv_w, conv_b, dt_bias, a_log, d_skip, ssm_norm_w):
    dt_col = 4 * HG_WIDTH + SSM_INNER + SSM_CONV_CH
    w_dt = jnp.pad(w_in[:, dt_col:], ((0, 0), (0, LANES - SSM_HEADS))).astype(BF16)
    main, aux = _ab_proj(h2, nw_pre, w_in.astype(BF16), w_dt)
    main3 = main.reshape(bsz, s, AB_MAIN_COLS)
    aux3 = aux.reshape(bsz, s, AB_AUX_COLS)
    o_a = _hgrn2(main3, aux3, hgrn_lb, hg_norm_w, layer_idx)
    o_b = _ssd(main3, aux3, conv_w, conv_b, dt_bias, a_log, d_skip, ssm_norm_w)
    t = bsz * s
    return _outproj_res([o_a.reshape(t, HG_WIDTH), o_b.reshape(t, SSM_INNER)], w_out.astype(BF16), nw_post, h2)


def _mixer_c(h2, bsz, s, nw_pre, nw_post, w_in, w_out):
    dm = h2.shape[1]
    x3 = h2.reshape(bsz, s, dm)
    qkvs = _qkv_proj(x3, nw_pre, w_in.astype(BF16))
    os, sts = [], []
    for g, (window, dilation) in enumerate(ATT_PATTERNS):
        o, st = _attn_group(qkvs[g], window // dilation)
        os.append(o)
        sts.append(st)
    return _attn_out(os, sts, w_out.astype(BF16), nw_post, x3).reshape(bsz * s, dm)


def kernel(x, norm_pre, norm_post, ffn_w_gate, ffn_w_up, ffn_w_down, ab_w_in, ab_w_out, hgrn_lb, hgrn_norm_w,
           ssm_conv_w, ssm_conv_b, ssm_dt_bias, ssm_A_log, ssm_D, ssm_norm_w, att_w_in, att_w_out):
    bsz, s, d = x.shape
    depth = norm_pre.shape[0]
    h2 = x.reshape(bsz * s, d)

    wg, wu, wd = ffn_w_gate.astype(BF16), ffn_w_up.astype(BF16), ffn_w_down.astype(BF16)

    def ffn(h2, layer, k):
        return _ffn(h2, norm_pre[layer, 2 * k], norm_post[layer, 2 * k], wg, wu, wd, layer, k)

    for layer in range(depth):
        h2 = ffn(h2, layer, 0)
        if layer % 2 == 0:
            e = layer // 2
            h2 = _mixer_ab(h2, bsz, s, norm_pre[layer, 1], norm_post[layer, 1], ab_w_in[e], ab_w_out[e],
                           hgrn_lb, e, hgrn_norm_w[e], ssm_conv_w[e], ssm_conv_b[e], ssm_dt_bias[e],
                           ssm_A_log[e], ssm_D[e], ssm_norm_w[e])
        else:
            o_idx = layer // 2
            h2 = _mixer_c(h2, bsz, s, norm_pre[layer, 1], norm_post[layer, 1], att_w_in[o_idx], att_w_out[o_idx])
        h2 = ffn(h2, layer, 1)
    return h2.reshape(bsz, s, d)
```
